```python
import math
import jax, jax.numpy as jnp
from jax import lax
import numpy as np

D_MODEL = 1024
BATCH = 8
SEQ = 2048
DEPTH = 4
DEC_BATCH = 32
DEC_SEQ = 1
PAST_LEN = 8192
PAGE_SIZE = 128

D_ATTN = D_MODEL // 2
D_REC = D_MODEL - D_ATTN
N_HEADS_A = 4
HD_A = D_ATTN // (2 * N_HEADS_A)
DV_A = 2 * HD_A
N_BLK_R = 8
BLK_R = D_REC // N_BLK_R
CONV_W = 4
LRU_C = 8.0
D_FF = ((8 * D_MODEL // 3 + 127) // 128) * 128
D_IN = 3 * D_ATTN + 2 * D_REC
QBLOCK = 128
ALPHA = (2 * DEPTH) ** 0.25
BETA = (8 * DEPTH) ** -0.25
LN_EPS = 1e-5
NEG_INF = -1e30

kernel_name = "hymba_diffattn_rglru_macaron_deepnorm_step"


def layer_norm(x, g, b):
    xf = x.astype(jnp.float32)
    mu = jnp.mean(xf, -1, keepdims=True)
    var = jnp.mean(jnp.square(xf - mu), -1, keepdims=True)
    return ((xf - mu) * lax.rsqrt(var + LN_EPS) * g + b).astype(x.dtype)


def rms_norm(x, g):
    return x * lax.rsqrt(jnp.mean(jnp.square(x), -1, keepdims=True) + LN_EPS) * g


def swiglu(x, w_gate, w_up, w_down):
    return (jax.nn.silu(x @ w_gate) * (x @ w_up)) @ w_down


def diff_attn_core(q, k, v, q_pos, k_pos, lam):
    B, Tk = k.shape[:2]
    k = k.reshape(B, Tk, N_HEADS_A, 2, HD_A).astype(jnp.float32)
    s = jnp.einsum("bqhcd,bkhcd->bhcqk", q.astype(jnp.float32), k) * (HD_A ** -0.5)
    causal = k_pos[None, :] <= q_pos[:, None]
    s = jnp.where(causal, s, NEG_INF)
    p = jax.nn.softmax(s, axis=-1)
    w = p[:, :, 0] - lam * p[:, :, 1]
    return jnp.einsum("bhqk,bkhv->bqhv", w, v.astype(jnp.float32))


def prompt_attention(q, k, v, lam):
    B, T = q.shape[:2]
    nb = T // QBLOCK
    qb = q.reshape(B, nb, QBLOCK, N_HEADS_A, 2, HD_A).swapaxes(0, 1)
    k_pos = jnp.arange(T)

    def block(args):
        qi, i = args
        return diff_attn_core(qi, k, v, i * QBLOCK + jnp.arange(QBLOCK), k_pos, lam)

    o = lax.map(block, (qb, jnp.arange(nb)))
    return o.swapaxes(0, 1).reshape(B, T, N_HEADS_A, DV_A)


def paged_attention(q, k, v, lam, cache_k_l, cache_v_l, page_table):
    B, T = q.shape[:2]
    past_k = cache_k_l[page_table].reshape(B, -1, N_HEADS_A, 2 * HD_A)
    past_v = cache_v_l[page_table].reshape(B, -1, N_HEADS_A, DV_A)
    past = past_k.shape[1]
    k_all = jnp.concatenate([past_k.astype(k.dtype), k], axis=1)
    v_all = jnp.concatenate([past_v.astype(v.dtype), v], axis=1)
    q_pos = past + jnp.arange(T)
    k_pos = jnp.arange(past + T)
    return diff_attn_core(q, k_all, v_all, q_pos, k_pos, lam)


def rglru_branch(xr, xg, conv_buf, h0, conv_w, conv_b, ga_w, ga_b, gx_w, gx_b, lru_lambda):
    B, T, _ = xr.shape
    xpad = jnp.concatenate([conv_buf.astype(xr.dtype), xr], axis=1)
    xc = conv_b + sum(xpad[:, j:j + T] * conv_w[j] for j in range(CONV_W))
    new_buf = xpad[:, T:]
    xb = xc.reshape(B, T, N_BLK_R, BLK_R)
    r = jax.nn.sigmoid(jnp.einsum("btnd,nde->btne", xb, ga_w).reshape(B, T, D_REC).astype(jnp.float32) + ga_b)
    i = jax.nn.sigmoid(jnp.einsum("btnd,nde->btne", xb, gx_w).reshape(B, T, D_REC).astype(jnp.float32) + gx_b)
    log_a = -LRU_C * r * jax.nn.softplus(-lru_lambda.astype(jnp.float32))
    a = jnp.exp(log_a)
    b = jnp.sqrt(-jnp.expm1(2.0 * log_a)) * i * xc.astype(jnp.float32)

    def step(h, ab):
        a_t, b_t = ab
        h = a_t * h + b_t
        return h, h

    h_last, hs = lax.scan(step, h0.astype(jnp.float32), (a.swapaxes(0, 1), b.swapaxes(0, 1)))
    y = hs.swapaxes(0, 1) * jax.nn.gelu(xg.astype(jnp.float32))
    return y.astype(xr.dtype), new_buf, h_last.astype(xr.dtype)


def run_group(x, attend, conv_bufs, h0s, P):
    B, T, _ = x.shape
    ks, vs, convs, hs = [], [], [], []
    for l in range(DEPTH):
        lam_init = 0.8 - 0.6 * math.exp(-0.3 * l)
        x = layer_norm(ALPHA * x + 0.5 * swiglu(x, P["ffn1_w_gate"][l], P["ffn1_w_up"][l], P["ffn1_w_down"][l]),
                       P["ln1_g"][l], P["ln1_b"][l])
        proj = x @ P["w_in"][l]
        q, k, v, xr, xg = jnp.split(proj, [D_ATTN, 2 * D_ATTN, 3 * D_ATTN, 3 * D_ATTN + D_REC], axis=-1)
        q = q.reshape(B, T, N_HEADS_A, 2, HD_A)
        k = k.reshape(B, T, N_HEADS_A, 2 * HD_A)
        v = v.reshape(B, T, N_HEADS_A, DV_A)
        lam = (jnp.exp(jnp.sum(P["attn_lam_q1"][l].astype(jnp.float32) * P["attn_lam_k1"][l]))
               - jnp.exp(jnp.sum(P["attn_lam_q2"][l].astype(jnp.float32) * P["attn_lam_k2"][l]))
               + lam_init)
        o = attend(l, q, k, v, lam)
        o = rms_norm(o, P["attn_subln_g"][l]) * (1.0 - lam_init)
        attn_out = o.reshape(B, T, D_ATTN).astype(x.dtype)
        rec_out, conv_new, h_new = rglru_branch(
            xr, xg, conv_bufs[l], h0s[l], P["conv_w"][l], P["conv_b"][l],
            P["gate_a_w"][l], P["gate_a_b"][l], P["gate_x_w"][l], P["gate_x_b"][l], P["lru_lambda"][l])
        mix = jnp.concatenate([attn_out, rec_out], axis=-1) @ P["w_out"][l]
        x = layer_norm(ALPHA * x + mix, P["ln2_g"][l], P["ln2_b"][l])
        x = layer_norm(ALPHA * x + 0.5 * swiglu(x, P["ffn2_w_gate"][l], P["ffn2_w_up"][l], P["ffn2_w_down"][l]),
                       P["ln3_g"][l], P["ln3_b"][l])
        ks.append(k)
        vs.append(v)
        convs.append(conv_new)
        hs.append(h_new)
    return x, jnp.stack(ks), jnp.stack(vs), jnp.stack(convs), jnp.stack(hs)


def setup_inputs(seed: int = 0) -> dict:
    key = jax.random.key(seed)
    ks = jax.random.split(key, 40)
    f32 = jnp.float32

    def nrm(i, shape, scale):
        return jax.random.normal(ks[i], shape, f32) * scale

    def gain(i, shape):
        return 1.0 + 0.02 * jax.random.normal(ks[i], shape, f32)

    n_pages = PAST_LEN // PAGE_SIZE
    n_used = DEC_BATCH * n_pages
    n_pool = n_used + max(1, n_used // 4)
    page_table = jax.random.permutation(ks[0], n_pool)[:n_used].reshape(DEC_BATCH, n_pages).astype(jnp.int32)

    col_scale = jnp.concatenate([jnp.ones((2 * D_ATTN,), f32), jnp.full((D_ATTN + D_REC,), BETA, f32),
                                 jnp.ones((D_REC,), f32)])
    a0 = jax.random.uniform(ks[1], (DEPTH, D_REC), f32, minval=0.9, maxval=0.999)

    return {
        "x_prompt": nrm(2, (BATCH, SEQ, D_MODEL), 1.0),
        "x_sample": nrm(3, (DEC_BATCH, DEC_SEQ, D_MODEL), 1.0),
        "cache_k": nrm(4, (DEPTH, n_pool, PAGE_SIZE, N_HEADS_A, 2 * HD_A), 1.0),
        "cache_v": nrm(5, (DEPTH, n_pool, PAGE_SIZE, N_HEADS_A, DV_A), BETA),
        "state_conv": nrm(6, (DEPTH, DEC_BATCH, CONV_W - 1, D_REC), BETA),
        "state_h": nrm(7, (DEPTH, DEC_BATCH, D_REC), 0.5),
        "page_table": page_table,
        "ln1_g": gain(8, (DEPTH, D_MODEL)),
        "ln1_b": nrm(9, (DEPTH, D_MODEL), 0.02),
        "ln2_g": gain(10, (DEPTH, D_MODEL)),
        "ln2_b": nrm(11, (DEPTH, D_MODEL), 0.02),
        "ln3_g": gain(12, (DEPTH, D_MODEL)),
        "ln3_b": nrm(13, (DEPTH, D_MODEL), 0.02),
        "ffn1_w_gate": nrm(14, (DEPTH, D_MODEL, D_FF), BETA * D_MODEL ** -0.5),
        "ffn1_w_up": nrm(15, (DEPTH, D_MODEL, D_FF), BETA * D_MODEL ** -0.5),
        "ffn1_w_down": nrm(16, (DEPTH, D_FF, D_MODEL), BETA * D_FF ** -0.5),
        "ffn2_w_gate": nrm(17, (DEPTH, D_MODEL, D_FF), BETA * D_MODEL ** -0.5),
        "ffn2_w_up": nrm(18, (DEPTH, D_MODEL, D_FF), BETA * D_MODEL ** -0.5),
        "ffn2_w_down": nrm(19, (DEPTH, D_FF, D_MODEL), BETA * D_FF ** -0.5),
        "w_in": nrm(20, (DEPTH, D_MODEL, D_IN), D_MODEL ** -0.5) * col_scale,
        "w_out": nrm(21, (DEPTH, D_MODEL, D_MODEL), BETA * D_MODEL ** -0.5),
        "attn_lam_q1": nrm(22, (DEPTH, HD_A), 0.1),
        "attn_lam_k1": nrm(23, (DEPTH, HD_A), 0.1),
        "attn_lam_q2": nrm(24, (DEPTH, HD_A), 0.1),
        "attn_lam_k2": nrm(25, (DEPTH, HD_A), 0.1),
        "attn_subln_g": gain(26, (DEPTH, DV_A)),
        "conv_w": nrm(27, (DEPTH, CONV_W, D_REC), CONV_W ** -0.5),
        "conv_b": nrm(28, (DEPTH, D_REC), 0.02),
        "gate_a_w": nrm(29, (DEPTH, N_BLK_R, BLK_R, BLK_R), BLK_R ** -0.5),
        "gate_a_b": nrm(30, (DEPTH, D_REC), 0.02),
        "gate_x_w": nrm(31, (DEPTH, N_BLK_R, BLK_R, BLK_R), BLK_R ** -0.5),
        "gate_x_b": nrm(32, (DEPTH, D_REC), 0.02),
        "lru_lambda": jnp.log(a0) - jnp.log1p(-a0),
    }


def reference(x_prompt, x_sample, cache_k, cache_v, state_conv, state_h, page_table,
              ln1_g, ln1_b, ln2_g, ln2_b, ln3_g, ln3_b,
              ffn1_w_gate, ffn1_w_up, ffn1_w_down, ffn2_w_gate, ffn2_w_up, ffn2_w_down,
              w_in, w_out, attn_lam_q1, attn_lam_k1, attn_lam_q2, attn_lam_k2, attn_subln_g,
              conv_w, conv_b, gate_a_w, gate_a_b, gate_x_w, gate_x_b, lru_lambda):
    P = {
        "ln1_g": ln1_g, "ln1_b": ln1_b, "ln2_g": ln2_g, "ln2_b": ln2_b, "ln3_g": ln3_g, "ln3_b": ln3_b,
        "ffn1_w_gate": ffn1_w_gate, "ffn1_w_up": ffn1_w_up, "ffn1_w_down": ffn1_w_down,
        "ffn2_w_gate": ffn2_w_gate, "ffn2_w_up": ffn2_w_up, "ffn2_w_down": ffn2_w_down,
        "w_in": w_in, "w_out": w_out,
        "attn_lam_q1": attn_lam_q1, "attn_lam_k1": attn_lam_k1,
        "attn_lam_q2": attn_lam_q2, "attn_lam_k2": attn_lam_k2, "attn_subln_g": attn_subln_g,
        "conv_w": conv_w, "conv_b": conv_b, "gate_a_w": gate_a_w, "gate_a_b": gate_a_b,
        "gate_x_w": gate_x_w, "gate_x_b": gate_x_b, "lru_lambda": lru_lambda,
    }
    B = x_prompt.shape[0]
    conv0 = jnp.zeros((DEPTH, B, CONV_W - 1, D_REC), x_prompt.dtype)
    h0 = jnp.zeros((DEPTH, B, D_REC), x_prompt.dtype)
    y_prompt, k_prompt, v_prompt, conv_prompt, h_prompt = run_group(
        x_prompt, lambda l, q, k, v, lam: prompt_attention(q, k, v, lam), conv0, h0, P)
    y_sample, k_sample, v_sample, conv_sample, h_sample = run_group(
        x_sample, lambda l, q, k, v, lam: paged_attention(q, k, v, lam, cache_k[l], cache_v[l], page_table),
        state_conv, state_h, P)
    return (y_prompt, y_sample, k_prompt, v_prompt, conv_prompt, h_prompt,
            k_sample, v_sample, conv_sample, h_sample)
```

```python
import functools
import math

import jax
import jax.numpy as jnp
from jax import lax
from jax.experimental import pallas as pl
from jax.experimental.pallas import tpu as pltpu

F32 = jnp.float32
BF16 = jnp.bfloat16

LANES = 128
N_HEADS = 4
HEAD_DIM = 64
HEAD_W = 2 * HEAD_DIM
CONV_W = 4
LRU_C = 8.0
LN_EPS = 1e-5
NEG_INF = -1e30
QK_SCALE = HEAD_DIM ** -0.5

VMEM_LIMIT_BYTES = 56 * 1024 * 1024


def _compiler_params(semantics):
    return pltpu.CompilerParams(dimension_semantics=semantics, vmem_limit_bytes=VMEM_LIMIT_BYTES)


def _resident(shape):
    return pl.BlockSpec(shape, lambda *_: (0,) * len(shape), pipeline_mode=pl.Buffered(1))


def _layer_norm(y, g, b):
    mu = jnp.mean(y, axis=-1, keepdims=True)
    d = y - mu
    var = jnp.mean(d * d, axis=-1, keepdims=True)
    return d * lax.rsqrt(var + LN_EPS) * g + b


def _dot(a, b):
    return jnp.dot(a, b, preferred_element_type=F32)


def _ffn_ln_kernel(x_ref, wg_ref, wu_ref, wd_ref, g_ref, b_ref, o_ref, *, alpha):
    x = x_ref[...]
    xb = x.astype(BF16)
    gate = _dot(xb, wg_ref[...])
    up = _dot(xb, wu_ref[...])
    h = (gate * jax.nn.sigmoid(gate) * up).astype(BF16)
    f = _dot(h, wd_ref[...])
    o_ref[...] = _layer_norm(alpha * x + 0.5 * f, g_ref[...], b_ref[...])


def _ffn_ln(x, wg, wu, wd, g, b, *, alpha, tm):
    n, d = x.shape
    d_ff = wg.shape[1]
    return pl.pallas_call(
        functools.partial(_ffn_ln_kernel, alpha=alpha),
        grid=(n // tm,),
        in_specs=[
            pl.BlockSpec((tm, d), lambda i: (i, 0)),
            _resident((d, d_ff)), _resident((d, d_ff)), _resident((d_ff, d)),
            _resident((1, d)), _resident((1, d)),
        ],
        out_specs=pl.BlockSpec((tm, d), lambda i: (i, 0)),
        out_shape=jax.ShapeDtypeStruct((n, d), F32),
        compiler_params=_compiler_params(("parallel",)),
        name="ffn_ln",
    )(x, wg, wu, wd, g, b)


def _in_proj_kernel(x_ref, w_ref, q_ref, k_ref, v_ref, kb_ref, vb_ref, xr_ref, xg_ref, *, d_attn, d_rec):
    proj = _dot(x_ref[...].astype(BF16), w_ref[...])
    q = proj[:, :d_attn]
    k = proj[:, d_attn:2 * d_attn]
    v = proj[:, 2 * d_attn:3 * d_attn]
    q_ref[...] = (q * QK_SCALE).astype(BF16)
    k_ref[...] = k
    v_ref[...] = v
    kb_ref[...] = k.astype(BF16)
    vb_ref[...] = v.astype(BF16)
    xr_ref[...] = proj[:, 3 * d_attn:3 * d_attn + d_rec]
    xg_ref[...] = proj[:, 3 * d_attn + d_rec:]


def _in_proj(x, w_in, *, d_attn, d_rec, tm):
    n, d = x.shape
    d_in = w_in.shape[1]
    row = lambda w: pl.BlockSpec((tm, w), lambda i: (i, 0))
    sds = lambda w, dt: jax.ShapeDtypeStruct((n, w), dt)
    return pl.pallas_call(
        functools.partial(_in_proj_kernel, d_attn=d_attn, d_rec=d_rec),
        grid=(n // tm,),
        in_specs=[row(d), _resident((d, d_in))],
        out_specs=[row(d_attn), row(d_attn), row(d_attn), row(d_attn), row(d_attn), row(d_rec), row(d_rec)],
        out_shape=[sds(d_attn, BF16), sds(d_attn, F32), sds(d_attn, F32), sds(d_attn, BF16), sds(d_attn, BF16),
                   sds(d_rec, F32), sds(d_rec, F32)],
        compiler_params=_compiler_params(("parallel",)),
        name="in_proj",
    )(x, w_in)


def _lambda(lq1_ref, lk1_ref, lq2_ref, lk2_ref, lam_init):
    s1 = jnp.sum(lq1_ref[...] * lk1_ref[...], axis=-1, keepdims=True)
    s2 = jnp.sum(lq2_ref[...] * lk2_ref[...], axis=-1, keepdims=True)
    return jnp.exp(s1) - jnp.exp(s2) + lam_init


def _sub_ln(o, g, lam_init):
    ms = jnp.mean(o * o, axis=-1, keepdims=True)
    return o * lax.rsqrt(ms + LN_EPS) * g * (1.0 - lam_init)


def _prompt_attn_kernel(lq1_ref, lk1_ref, lq2_ref, lk2_ref, g_ref, q_ref, k_ref, v_ref, o_ref,
                        m_ref, l_ref, acc_ref, *, lam_init, tq):
    qi = pl.program_id(2)
    q = q_ref[...]
    lane = lax.broadcasted_iota(jnp.int32, q.shape, 1)
    zero = jnp.zeros_like(q)
    qs = jnp.concatenate([jnp.where(lane < HEAD_DIM, q, zero),
                          jnp.where(lane >= HEAD_DIM, q, zero)], axis=0)

    m_ref[...] = jnp.full(m_ref.shape, NEG_INF, F32)
    l_ref[...] = jnp.zeros(l_ref.shape, F32)
    acc_ref[...] = jnp.zeros(acc_ref.shape, F32)

    def block(kb, masked):
        start = pl.multiple_of(kb * tq, tq)
        k = k_ref[pl.ds(start, tq), :]
        v = v_ref[pl.ds(start, tq), :]
        s = lax.dot_general(qs, k, (((1,), (1,)), ((), ())), preferred_element_type=F32)
        if masked:
            row = lax.broadcasted_iota(jnp.int32, (tq, tq), 0)
            col = lax.broadcasted_iota(jnp.int32, (tq, tq), 1)
            keep = jnp.concatenate([col <= row, col <= row], axis=0)
            s = jnp.where(keep, s, NEG_INF)
        m_prev = m_ref[...]
        m_new = jnp.maximum(m_prev, jnp.max(s, axis=-1, keepdims=True))
        corr = jnp.exp(m_prev - m_new)
        p = jnp.exp(s - m_new)
        l_ref[...] = corr * l_ref[...] + jnp.sum(p, axis=-1, keepdims=True)
        acc_ref[...] = corr * acc_ref[...] + _dot(p.astype(BF16), v)
        m_ref[...] = m_new

    def body(kb, carry):
        block(kb, masked=False)
        return carry

    lax.fori_loop(0, qi, body, 0)
    block(qi, masked=True)

    lam = _lambda(lq1_ref, lk1_ref, lq2_ref, lk2_ref, lam_init)
    o_all = acc_ref[...] / l_ref[...]
    o = o_all[:tq] - lam * o_all[tq:]
    o_ref[...] = _sub_ln(o, g_ref[...], lam_init).astype(o_ref.dtype)


def _prompt_attn(q, kb, vb, lq1, lk1, lq2, lk2, g, *, lam_init, batch, seq, tq):
    n, d_attn = q.shape
    nq = seq // tq
    small = lambda a: _resident(a.shape)
    return pl.pallas_call(
        functools.partial(_prompt_attn_kernel, lam_init=lam_init, tq=tq),
        grid=(batch, N_HEADS, nq),
        in_specs=[
            small(lq1), small(lk1), small(lq2), small(lk2), small(g),
            pl.BlockSpec((tq, HEAD_W), lambda b, h, i: (b * nq + i, h)),
            pl.BlockSpec((seq, HEAD_W), lambda b, h, i: (b, h)),
            pl.BlockSpec((seq, HEAD_W), lambda b, h, i: (b, h)),
        ],
        out_specs=pl.BlockSpec((tq, HEAD_W), lambda b, h, i: (b * nq + i, h)),
        out_shape=jax.ShapeDtypeStruct((n, d_attn), BF16),
        scratch_shapes=[pltpu.VMEM((2 * tq, 1), F32), pltpu.VMEM((2 * tq, 1), F32),
                        pltpu.VMEM((2 * tq, HEAD_W), F32)],
        compiler_params=_compiler_params(("parallel", "parallel", "arbitrary")),
        name="prompt_attn",
    )(lq1, lk1, lq2, lk2, g, q, kb, vb)


def _decode_attn_kernel(pt_ref, lq1_ref, lk1_ref, lq2_ref, lk2_ref, g_ref, q_ref, kn_ref, vn_ref, *rest,
                        lam_init, pages_per_step):
    del pt_ref
    k_refs = rest[:pages_per_step]
    v_refs = rest[pages_per_step:2 * pages_per_step]
    o_ref, m_ref, l_ref, acc_ref = rest[2 * pages_per_step:]
    step = pl.program_id(1)
    n_rows = 2 * N_HEADS
    d_attn = N_HEADS * HEAD_W

    row = lax.broadcasted_iota(jnp.int32, (n_rows, d_attn), 0)
    lane = lax.broadcasted_iota(jnp.int32, (n_rows, d_attn), 1)
    own = (lane // HEAD_W == row % N_HEADS) & ((lane % HEAD_W) // HEAD_DIM == row // N_HEADS)
    qm_f32 = jnp.where(own, jnp.broadcast_to(q_ref[...].astype(F32), (n_rows, d_attn)), 0.0)
    qm = qm_f32.astype(BF16)

    @pl.when(step == 0)
    def _():
        m_ref[...] = jnp.full(m_ref.shape, NEG_INF, F32)
        l_ref[...] = jnp.zeros(l_ref.shape, F32)
        acc_ref[...] = jnp.zeros(acc_ref.shape, F32)

    def online_update(s, pv_of):
        m_prev = m_ref[...]
        m_new = jnp.maximum(m_prev, jnp.max(s, axis=-1, keepdims=True))
        corr = jnp.exp(m_prev - m_new)
        p = jnp.exp(s - m_new)
        l_ref[...] = corr * l_ref[...] + jnp.sum(p, axis=-1, keepdims=True)
        acc_ref[...] = corr * acc_ref[...] + pv_of(p)
        m_ref[...] = m_new

    k_cat = jnp.concatenate([r[...].astype(BF16) for r in k_refs], axis=0)
    v_cat = jnp.concatenate([r[...].astype(BF16) for r in v_refs], axis=0)
    s_past = lax.dot_general(qm, k_cat, (((1,), (1,)), ((), ())), preferred_element_type=F32)
    online_update(s_past, lambda p: _dot(p.astype(BF16), v_cat))

    @pl.when(step == pl.num_programs(1) - 1)
    def _():
        kn = kn_ref[...].astype(BF16).astype(F32)
        vn = vn_ref[...].astype(BF16).astype(F32)
        s_self = jnp.sum(qm_f32 * kn, axis=-1, keepdims=True)
        online_update(s_self, lambda p: p.astype(BF16).astype(F32) * vn)

        lam = _lambda(lq1_ref, lk1_ref, lq2_ref, lk2_ref, lam_init)
        o_all = acc_ref[...] / l_ref[...]
        g = g_ref[...]
        heads = []
        for h in range(N_HEADS):
            lanes = slice(h * HEAD_W, (h + 1) * HEAD_W)
            o = o_all[h:h + 1, lanes] - lam * o_all[N_HEADS + h:N_HEADS + h + 1, lanes]
            heads.append(_sub_ln(o, g, lam_init))
        o_ref[...] = jnp.concatenate(heads, axis=-1).astype(o_ref.dtype)


def _decode_attn(q, k_new, v_new, cache_k, cache_v, page_table, lq1, lk1, lq2, lk2, g, *,
                 layer, lam_init, pages_per_step):
    n_seq, n_pages = page_table.shape
    page, d_attn = cache_k.shape[2:]
    n_steps = n_pages // pages_per_step
    small = lambda a: pl.BlockSpec(a.shape, lambda s, p, pt: (0,) * a.ndim)
    per_seq = pl.BlockSpec((None, 1, d_attn), lambda s, p, pt: (s, 0, 0))

    def page_spec(j):
        return pl.BlockSpec((None, None, page, d_attn),
                            lambda s, p, pt: (layer, pt[s, p * pages_per_step + j], 0, 0))

    pages = [page_spec(j) for j in range(pages_per_step)]
    return pl.pallas_call(
        functools.partial(_decode_attn_kernel, lam_init=lam_init, pages_per_step=pages_per_step),
        grid_spec=pltpu.PrefetchScalarGridSpec(
            num_scalar_prefetch=1,
            grid=(n_seq, n_steps),
            in_specs=[small(lq1), small(lk1), small(lq2), small(lk2), small(g), per_seq, per_seq, per_seq]
                     + pages + pages,
            out_specs=per_seq,
            scratch_shapes=[pltpu.VMEM((2 * N_HEADS, 1), F32), pltpu.VMEM((2 * N_HEADS, 1), F32),
                            pltpu.VMEM((2 * N_HEADS, d_attn), F32)],
        ),
        out_shape=jax.ShapeDtypeStruct((n_seq, 1, d_attn), BF16),
        compiler_params=_compiler_params(("parallel", "arbitrary")),
        name="decode_attn",
    )(page_table, lq1, lk1, lq2, lk2, g, q, k_new, v_new,
      *([cache_k] * pages_per_step), *([cache_v] * pages_per_step))


def _gelu_tanh(x):
    return 0.5 * x * (1.0 + jnp.tanh(math.sqrt(2.0 / math.pi) * (x + 0.044715 * (x * x * x))))


def _lru_coefficients(xc, wa_ref, ba_ref, wx_ref, bx_ref, lam_ref):
    xcb = xc.astype(BF16)
    r = jax.nn.sigmoid(_dot(xcb, wa_ref[...]) + ba_ref[...])
    i = jax.nn.sigmoid(_dot(xcb, wx_ref[...]) + bx_ref[...])
    log_a = -LRU_C * r * jax.nn.softplus(-lam_ref[...])
    a = jnp.exp(log_a)
    b = jnp.sqrt(-jnp.tanh(log_a) * (a * a + 1.0)) * i * xc
    return a, b


def _rglru_seq_kernel(xr_ref, xg_ref, conv0_ref, h0_ref, cw_ref, cb_ref, wa_ref, ba_ref, wx_ref, bx_ref, lam_ref,
                      y_ref, convn_ref, hn_ref, xs_ref, a_ref, b_ref, h_ref, *, tt, nb):
    t = pl.program_id(0)
    d_rec = xr_ref.shape[-1]
    pad = 8
    hist = CONV_W - 1

    @pl.when(t == 0)
    def _():
        xs_ref[:, pad - hist:pad, :] = conv0_ref[...]
        h_ref[...] = h0_ref[...]

    @pl.when(t > 0)
    def _():
        xs_ref[:, pad - hist:pad, :] = xs_ref[:, pad + tt - hist:pad + tt, :]

    xs_ref[:, pad:, :] = xr_ref[...]
    xc = cb_ref[...][None]
    for j in range(CONV_W):
        xc = xc + xs_ref[:, pad - hist + j:pad - hist + j + tt, :] * cw_ref[j:j + 1, :][None]
    xc = xc.reshape(nb * tt, d_rec)

    a, b = _lru_coefficients(xc, wa_ref, ba_ref, wx_ref, bx_ref, lam_ref)
    n_chunks = d_rec // LANES
    for c in range(n_chunks):
        a_ref[c] = a[:, c * LANES:(c + 1) * LANES]
        b_ref[c] = b[:, c * LANES:(c + 1) * LANES]

    def step(s, hs):
        rows = pl.ds(s, nb, stride=tt)
        new = []
        for c in range(n_chunks):
            h = a_ref[c, rows, :] * hs[c] + b_ref[c, rows, :]
            b_ref[c, rows, :] = h
            new.append(h)
        return tuple(new)

    h0 = h_ref[...]
    hs = lax.fori_loop(0, tt, step, tuple(h0[:, c * LANES:(c + 1) * LANES] for c in range(n_chunks)), unroll=8)
    h = jnp.concatenate(hs, axis=-1)
    h_ref[...] = h

    y = jnp.concatenate([b_ref[c] for c in range(n_chunks)], axis=-1) * _gelu_tanh(
        xg_ref[...].reshape(nb * tt, d_rec))
    y_ref[...] = y.reshape(nb, tt, d_rec).astype(y_ref.dtype)
    convn_ref[...] = xs_ref[:, pad + tt - hist:pad + tt, :]
    hn_ref[...] = h


def _rglru_seq(xr, xg, conv0, h0, cw, cb, wa, ba, wx, bx, lam, *, tt):
    nb, seq, d_rec = xr.shape
    tile = pl.BlockSpec((nb, tt, d_rec), lambda t: (0, t, 0))
    whole = lambda a: pl.BlockSpec(a.shape, lambda t: (0,) * a.ndim)
    return pl.pallas_call(
        functools.partial(_rglru_seq_kernel, tt=tt, nb=nb),
        grid=(seq // tt,),
        in_specs=[tile, tile] + [whole(a) for a in (conv0, h0, cw, cb, wa, ba, wx, bx, lam)],
        out_specs=[tile, whole(conv0), whole(h0)],
        out_shape=[jax.ShapeDtypeStruct((nb, seq, d_rec), BF16),
                   jax.ShapeDtypeStruct(conv0.shape, F32), jax.ShapeDtypeStruct(h0.shape, F32)],
        scratch_shapes=[pltpu.VMEM((nb, 8 + tt, d_rec), F32), pltpu.VMEM((d_rec // LANES, nb * tt, LANES), F32),
                        pltpu.VMEM((d_rec // LANES, nb * tt, LANES), F32), pltpu.VMEM((nb, d_rec), F32)],
        compiler_params=_compiler_params(("arbitrary",)),
        name="rglru_seq",
    )(xr, xg, conv0, h0, cw, cb, wa, ba, wx, bx, lam)


def _rglru_step_kernel(xr_ref, xg_ref, conv_ref, h0_ref, cw_ref, cb_ref, wa_ref, ba_ref, wx_ref, bx_ref, lam_ref,
                       y_ref, convn_ref, hn_ref):
    xr = xr_ref[...]
    xc = cb_ref[...] + xr * cw_ref[CONV_W - 1:CONV_W, :]
    for j in range(CONV_W - 1):
        xc = xc + conv_ref[j] * cw_ref[j:j + 1, :]
    a, b = _lru_coefficients(xc, wa_ref, ba_ref, wx_ref, bx_ref, lam_ref)
    h = a * h0_ref[...] + b
    y_ref[...] = (h * _gelu_tanh(xg_ref[...])).astype(y_ref.dtype)
    for j in range(CONV_W - 2):
        convn_ref[j] = conv_ref[j + 1]
    convn_ref[CONV_W - 2] = xr
    hn_ref[...] = h


def _rglru_step(xr, xg, conv_t, h0, cw, cb, wa, ba, wx, bx, lam):
    args = (xr, xg, conv_t, h0, cw, cb, wa, ba, wx, bx, lam)
    whole = lambda a: pl.BlockSpec(a.shape, lambda: (0,) * a.ndim)
    return pl.pallas_call(
        _rglru_step_kernel,
        in_specs=[whole(a) for a in args],
        out_specs=[whole(xr), whole(conv_t), whole(h0)],
        out_shape=[jax.ShapeDtypeStruct(xr.shape, BF16), jax.ShapeDtypeStruct(conv_t.shape, F32),
                   jax.ShapeDtypeStruct(h0.shape, F32)],
        name="rglru_step",
    )(*args)


def _out_ln_kernel(x_ref, attn_ref, rec_ref, w_ref, g_ref, b_ref, o_ref, *, alpha):
    mixed = jnp.concatenate([attn_ref[...], rec_ref[...]], axis=-1)
    o_ref[...] = _layer_norm(alpha * x_ref[...] + _dot(mixed, w_ref[...]), g_ref[...], b_ref[...])


def _out_ln(x, attn, rec, w_out, g, b, *, alpha, tm):
    n, d = x.shape
    row = lambda w: pl.BlockSpec((tm, w), lambda i: (i, 0))
    return pl.pallas_call(
        functools.partial(_out_ln_kernel, alpha=alpha),
        grid=(n // tm,),
        in_specs=[row(d), row(attn.shape[1]), row(rec.shape[1]), _resident((d, d)),
                  _resident((1, d)), _resident((1, d))],
        out_specs=row(d),
        out_shape=jax.ShapeDtypeStruct((n, d), F32),
        compiler_params=_compiler_params(("parallel",)),
        name="out_ln",
    )(x, attn, rec, w_out, g, b)


def _block_diag(w):
    n_blk, blk, _ = w.shape
    eye = jnp.eye(n_blk, dtype=w.dtype)
    return jnp.einsum("nde,nm->ndme", w, eye).reshape(n_blk * blk, n_blk * blk)


def _row_tile(n, preferred):
    return preferred if n % preferred == 0 else n


def kernel(x_prompt, x_sample, cache_k, cache_v, state_conv, state_h, page_table, ln1_g, ln1_b, ln2_g, ln2_b, ln3_g, ln3_b, ffn1_w_gate, ffn1_w_up, ffn1_w_down, ffn2_w_gate, ffn2_w_up, ffn2_w_down, w_in, w_out, attn_lam_q1, attn_lam_k1, attn_lam_q2, attn_lam_k2, attn_subln_g, conv_w, conv_b, gate_a_w, gate_a_b, gate_x_w, gate_x_b, lru_lambda):
    depth = w_in.shape[0]
    batch, seq, d_model = x_prompt.shape
    n_seq = x_sample.shape[0]
    assert x_sample.shape[1] == 1, "the sample group decodes one token per sequence"
    d_attn = N_HEADS * HEAD_W
    d_rec = d_model - d_attn
    alpha = (2 * depth) ** 0.25
    n_pool, page = cache_k.shape[1:3]

    cache_k2 = cache_k.reshape(depth, n_pool, page, d_attn)
    cache_v2 = cache_v.reshape(depth, n_pool, page, d_attn)
    bf = lambda w: w.astype(BF16)
    vec = lambda p, l: p[l][None, :]

    xp = x_prompt.reshape(batch * seq, d_model)
    xs = x_sample.reshape(n_seq, d_model)
    conv_p = jnp.zeros((batch, CONV_W - 1, d_rec), F32)
    h_p = jnp.zeros((batch, d_rec), F32)

    tm = _row_tile(batch * seq, 256)
    tq = _row_tile(seq, 256)
    tt = _row_tile(seq, 128)
    pages_per_step = 8 if page_table.shape[1] % 8 == 0 else 1

    outs = {name: [] for name in ("kp", "vp", "cp", "hp", "ks", "vs", "cs", "hs")}
    for l in range(depth):
        lam_init = 0.8 - 0.6 * math.exp(-0.3 * l)
        ffn1 = (bf(ffn1_w_gate[l]), bf(ffn1_w_up[l]), bf(ffn1_w_down[l]), vec(ln1_g, l), vec(ln1_b, l))
        ffn2 = (bf(ffn2_w_gate[l]), bf(ffn2_w_up[l]), bf(ffn2_w_down[l]), vec(ln3_g, l), vec(ln3_b, l))
        w_in_l, w_out_l = bf(w_in[l]), bf(w_out[l])
        lam_vecs = (vec(attn_lam_q1, l), vec(attn_lam_k1, l), vec(attn_lam_q2, l), vec(attn_lam_k2, l))
        sub_g = vec(attn_subln_g, l)
        rec = (conv_w[l], vec(conv_b, l), bf(_block_diag(gate_a_w[l])), vec(gate_a_b, l),
               bf(_block_diag(gate_x_w[l])), vec(gate_x_b, l), vec(lru_lambda, l))

        xp = _ffn_ln(xp, *ffn1, alpha=alpha, tm=tm)
        q, k, v, kb, vb, xr, xg = _in_proj(xp, w_in_l, d_attn=d_attn, d_rec=d_rec, tm=tm)
        attn = _prompt_attn(q, kb, vb, *lam_vecs, sub_g, lam_init=lam_init, batch=batch, seq=seq, tq=tq)
        y, conv_n, h_n = _rglru_seq(xr.reshape(batch, seq, d_rec), xg.reshape(batch, seq, d_rec),
                                    conv_p, h_p, *rec, tt=tt)
        xp = _out_ln(xp, attn, y.reshape(batch * seq, d_rec), w_out_l, vec(ln2_g, l), vec(ln2_b, l),
                     alpha=alpha, tm=tm)
        xp = _ffn_ln(xp, *ffn2, alpha=alpha, tm=tm)
        outs["kp"].append(k.reshape(batch, seq, N_HEADS, HEAD_W))
        outs["vp"].append(v.reshape(batch, seq, N_HEADS, HEAD_W))
        outs["cp"].append(conv_n)
        outs["hp"].append(h_n)

        xs = _ffn_ln(xs, *ffn1, alpha=alpha, tm=n_seq)
        q, k, v, _, _, xr, xg = _in_proj(xs, w_in_l, d_attn=d_attn, d_rec=d_rec, tm=n_seq)
        attn = _decode_attn(q[:, None, :], k[:, None, :], v[:, None, :], cache_k2, cache_v2, page_table,
                            *lam_vecs, sub_g, layer=l, lam_init=lam_init, pages_per_step=pages_per_step)
        y, conv_n, h_n = _rglru_step(xr, xg, jnp.swapaxes(state_conv[l], 0, 1), state_h[l], *rec)
        xs = _out_ln(xs, attn.reshape(n_seq, d_attn), y, w_out_l, vec(ln2_g, l), vec(ln2_b, l),
                     alpha=alpha, tm=n_seq)
        xs = _ffn_ln(xs, *ffn2, alpha=alpha, tm=n_seq)
        outs["ks"].append(k.reshape(n_seq, 1, N_HEADS, HEAD_W))
        outs["vs"].append(v.reshape(n_seq, 1, N_HEADS, HEAD_W))
        outs["cs"].append(jnp.swapaxes(conv_n, 0, 1))
        outs["hs"].append(h_n)

    stack = lambda name: jnp.stack(outs[name])
    return (xp.reshape(batch, seq, d_model), xs.reshape(n_seq, 1, d_model),
            stack("kp"), stack("vp"), stack("cp"), stack("hp"),
            stack("ks"), stack("vs"), stack("cs"), stack("hs"))
```

```python
import functools
import math

import jax
import jax.numpy as jnp
from jax import lax
from jax.experimental import pallas as pl
from jax.experimental.pallas import tpu as pltpu

F32 = jnp.float32
BF16 = jnp.bfloat16

LANES = 128
N_HEADS = 4
HEAD_DIM = 64
HEAD_W = 2 * HEAD_DIM
CONV_W = 4
LRU_C = 8.0
LN_EPS = 1e-5
NEG_INF = -1e30
QK_SCALE = HEAD_DIM ** -0.5

VMEM_LIMIT_BYTES = 56 * 1024 * 1024


def _compiler_params(semantics):
    return pltpu.CompilerParams(dimension_semantics=semantics, vmem_limit_bytes=VMEM_LIMIT_BYTES)


def _resident(shape):
    return pl.BlockSpec(shape, lambda *_: (0,) * len(shape), pipeline_mode=pl.Buffered(1))


def _layer_norm(y, g, b):
    mu = jnp.mean(y, axis=-1, keepdims=True)
    d = y - mu
    var = jnp.mean(d * d, axis=-1, keepdims=True)
    return d * lax.rsqrt(var + LN_EPS) * g + b


def _dot(a, b):
    return jnp.dot(a, b, preferred_element_type=F32)


def _ffn_ln_kernel(x_ref, wg_ref, wu_ref, wd_ref, g_ref, b_ref, o_ref, *, alpha):
    x = x_ref[...]
    xb = x.astype(BF16)
    gate = _dot(xb, wg_ref[...])
    up = _dot(xb, wu_ref[...])
    h = (gate * jax.nn.sigmoid(gate) * up).astype(BF16)
    f = _dot(h, wd_ref[...])
    o_ref[...] = _layer_norm(alpha * x + 0.5 * f, g_ref[...], b_ref[...])


def _ffn_ln(x, wg, wu, wd, g, b, *, alpha, tm):
    n, d = x.shape
    d_ff = wg.shape[1]
    return pl.pallas_call(
        functools.partial(_ffn_ln_kernel, alpha=alpha),
        grid=(n // tm,),
        in_specs=[
            pl.BlockSpec((tm, d), lambda i: (i, 0)),
            _resident((d, d_ff)), _resident((d, d_ff)), _resident((d_ff, d)),
            _resident((1, d)), _resident((1, d)),
        ],
        out_specs=pl.BlockSpec((tm, d), lambda i: (i, 0)),
        out_shape=jax.ShapeDtypeStruct((n, d), F32),
        compiler_params=_compiler_params(("parallel",)),
        name="ffn_ln",
    )(x, wg, wu, wd, g, b)


def _in_proj_kernel(x_ref, w_ref, k_all_in, v_all_in, q_ref, k_all_ref, v_all_ref, kb_ref, vb_ref, xr_ref, xg_ref,
                    *, d_attn, d_rec):
    del k_all_in, v_all_in
    proj = _dot(x_ref[...].astype(BF16), w_ref[...])
    q = proj[:, :d_attn]
    k = proj[:, d_attn:2 * d_attn]
    v = proj[:, 2 * d_attn:3 * d_attn]
    q_ref[...] = (q * QK_SCALE).astype(BF16)
    for h in range(N_HEADS):
        k_all_ref[:, h, :] = k[:, h * HEAD_W:(h + 1) * HEAD_W]
        v_all_ref[:, h, :] = v[:, h * HEAD_W:(h + 1) * HEAD_W]
    kb_ref[...] = k.astype(BF16)
    vb_ref[...] = v.astype(BF16)
    xr_ref[...] = proj[:, 3 * d_attn:3 * d_attn + d_rec]
    xg_ref[...] = proj[:, 3 * d_attn + d_rec:]


def _in_proj(x, w_in, k_all, v_all, *, layer, d_attn, d_rec, tm):
    n, d = x.shape
    d_in = w_in.shape[1]
    row = lambda w: pl.BlockSpec((tm, w), lambda i: (i, 0))
    sds = lambda w, dt: jax.ShapeDtypeStruct((n, w), dt)
    layer_rows = pl.BlockSpec((None, tm, N_HEADS, HEAD_W), lambda i: (layer, i, 0, 0))
    in_hbm = pl.BlockSpec(memory_space=pl.ANY)
    return pl.pallas_call(
        functools.partial(_in_proj_kernel, d_attn=d_attn, d_rec=d_rec),
        grid=(n // tm,),
        in_specs=[row(d), _resident((d, d_in)), in_hbm, in_hbm],
        out_specs=[row(d_attn), layer_rows, layer_rows, row(d_attn), row(d_attn), row(d_rec), row(d_rec)],
        out_shape=[sds(d_attn, BF16), jax.ShapeDtypeStruct(k_all.shape, F32), jax.ShapeDtypeStruct(v_all.shape, F32),
                   sds(d_attn, BF16), sds(d_attn, BF16), sds(d_rec, F32), sds(d_rec, F32)],
        input_output_aliases={2: 1, 3: 2},
        compiler_params=_compiler_params(("parallel",)),
        name="in_proj",
    )(x, w_in, k_all, v_all)


def _lambda(lq1_ref, lk1_ref, lq2_ref, lk2_ref, lam_init):
    s1 = jnp.sum(lq1_ref[...] * lk1_ref[...], axis=-1, keepdims=True)
    s2 = jnp.sum(lq2_ref[...] * lk2_ref[...], axis=-1, keepdims=True)
    return jnp.exp(s1) - jnp.exp(s2) + lam_init


def _sub_ln(o, g, lam_init):
    ms = jnp.mean(o * o, axis=-1, keepdims=True)
    return o * lax.rsqrt(ms + LN_EPS) * g * (1.0 - lam_init)


def _stack_maps(q):
    lane = lax.broadcasted_iota(jnp.int32, q.shape, 1)
    zero = jnp.zeros_like(q)
    return jnp.concatenate([jnp.where(lane < HEAD_DIM, q, zero), jnp.where(lane >= HEAD_DIM, q, zero)], axis=0)


def _prompt_attn_kernel(lq1_ref, lk1_ref, lq2_ref, lk2_ref, g_ref, q_ref, k_ref, v_ref, o_ref,
                        s_ref, m_ref, l_ref, acc_ref, *, lam_init, tq, tk):
    qi = pl.program_id(2)
    qs = _stack_maps(q_ref[...])
    n_blk = ((qi + 1) * tq + tk - 1) // tk
    n_chunks = tk // LANES

    def scores(kb):
        k = k_ref[pl.ds(pl.multiple_of(kb * tk, tk), tk), :]
        return lax.dot_general(qs, k, (((1,), (1,)), ((), ())), preferred_element_type=F32)

    def fold(op, s):
        out = s[:, :LANES]
        for c in range(1, n_chunks):
            out = op(out, s[:, c * LANES:(c + 1) * LANES])
        return out

    m_ref[...] = jnp.full(m_ref.shape, NEG_INF, F32)

    def pass1(kb, carry):
        s = scores(kb)
        s_ref[kb] = s
        m_ref[...] = jnp.maximum(m_ref[...], fold(jnp.maximum, s))
        return carry

    lax.fori_loop(0, n_blk - 1, pass1, 0)
    last = n_blk - 1
    row = lax.broadcasted_iota(jnp.int32, (2 * tq, tk), 0)
    col = lax.broadcasted_iota(jnp.int32, (2 * tq, tk), 1)
    q_pos = qi * tq + jnp.where(row >= tq, row - tq, row)
    s = jnp.where(last * tk + col <= q_pos, scores(last), NEG_INF)
    s_ref[last] = s
    m = jnp.max(jnp.maximum(m_ref[...], fold(jnp.maximum, s)), axis=-1, keepdims=True)
    m_ref[...] = jnp.broadcast_to(m, m_ref.shape)

    l_ref[...] = jnp.zeros(l_ref.shape, F32)
    acc_ref[...] = jnp.zeros(acc_ref.shape, F32)

    def pass2(kb, carry):
        m_b = m_ref[...]
        s = s_ref[kb]
        p = jnp.concatenate([jnp.exp(s[:, c * LANES:(c + 1) * LANES] - m_b) for c in range(n_chunks)], axis=-1)
        l_ref[...] += fold(jnp.add, p)
        v = v_ref[pl.ds(pl.multiple_of(kb * tk, tk), tk), :]
        acc_ref[...] += _dot(p.astype(BF16), v)
        return carry

    lax.fori_loop(0, n_blk, pass2, 0)
    l = jnp.sum(l_ref[...], axis=-1, keepdims=True)

    lam = _lambda(lq1_ref, lk1_ref, lq2_ref, lk2_ref, lam_init)
    o_all = acc_ref[...] / l
    o = o_all[:tq] - lam * o_all[tq:]
    o_ref[...] = _sub_ln(o, g_ref[...], lam_init).astype(o_ref.dtype)


def _prompt_attn(q, kb, vb, lq1, lk1, lq2, lk2, g, *, lam_init, batch, seq, tq, tk):
    n, d_attn = q.shape
    nq = seq // tq
    small = lambda a: _resident(a.shape)
    return pl.pallas_call(
        functools.partial(_prompt_attn_kernel, lam_init=lam_init, tq=tq, tk=tk),
        grid=(batch, N_HEADS, nq),
        in_specs=[
            small(lq1), small(lk1), small(lq2), small(lk2), small(g),
            pl.BlockSpec((tq, HEAD_W), lambda b, h, i: (b * nq + i, h)),
            pl.BlockSpec((seq, HEAD_W), lambda b, h, i: (b, h)),
            pl.BlockSpec((seq, HEAD_W), lambda b, h, i: (b, h)),
        ],
        out_specs=pl.BlockSpec((tq, HEAD_W), lambda b, h, i: (b * nq + i, h)),
        out_shape=jax.ShapeDtypeStruct((n, d_attn), BF16),
        scratch_shapes=[pltpu.VMEM((seq // tk, 2 * tq, tk), F32), pltpu.VMEM((2 * tq, LANES), F32),
                        pltpu.VMEM((2 * tq, LANES), F32), pltpu.VMEM((2 * tq, HEAD_W), F32)],
        compiler_params=_compiler_params(("parallel", "parallel", "arbitrary")),
        name="prompt_attn",
    )(lq1, lk1, lq2, lk2, g, q, kb, vb)


def _decode_attn_kernel(pt_ref, lq1_ref, lk1_ref, lq2_ref, lk2_ref, g_ref, q_ref, kn_ref, vn_ref, *rest,
                        lam_init, pages_per_step):
    del pt_ref
    k_refs = rest[:pages_per_step]
    v_refs = rest[pages_per_step:2 * pages_per_step]
    o_ref, m_ref, l_ref, acc_ref = rest[2 * pages_per_step:]
    step = pl.program_id(1)

    qm_f32 = _stack_maps(q_ref[...].astype(F32))
    qm = qm_f32.astype(BF16)

    @pl.when(step == 0)
    def _():
        m_ref[...] = jnp.full(m_ref.shape, NEG_INF, F32)
        l_ref[...] = jnp.zeros(l_ref.shape, F32)
        acc_ref[...] = jnp.zeros(acc_ref.shape, F32)

    def online_update(s, pv_of):
        m_prev = m_ref[...]
        m_new = jnp.maximum(m_prev, jnp.max(s, axis=-1, keepdims=True))
        corr = jnp.exp(m_prev - m_new)
        p = jnp.exp(s - m_new)
        l_ref[...] = corr * l_ref[...] + jnp.sum(p, axis=-1, keepdims=True)
        acc_ref[...] = corr * acc_ref[...] + pv_of(p)
        m_ref[...] = m_new

    k_cat = jnp.concatenate([r[...].astype(BF16) for r in k_refs], axis=0)
    v_cat = jnp.concatenate([r[...].astype(BF16) for r in v_refs], axis=0)
    s_past = lax.dot_general(qm, k_cat, (((1,), (1,)), ((), ())), preferred_element_type=F32)
    row = lax.broadcasted_iota(jnp.int32, s_past.shape, 0)
    col = lax.broadcasted_iota(jnp.int32, s_past.shape, 1)
    s_past = jnp.where(col % N_HEADS == row % N_HEADS, s_past, NEG_INF)
    online_update(s_past, lambda p: _dot(p.astype(BF16), v_cat))

    @pl.when(step == pl.num_programs(1) - 1)
    def _():
        kn = kn_ref[...].astype(BF16).astype(F32)
        vn = vn_ref[...].astype(BF16).astype(F32)
        s_self = jnp.sum(qm_f32 * jnp.concatenate([kn, kn], axis=0), axis=-1, keepdims=True)
        online_update(s_self, lambda p: p.astype(BF16).astype(F32) * jnp.concatenate([vn, vn], axis=0))

        lam = _lambda(lq1_ref, lk1_ref, lq2_ref, lk2_ref, lam_init)
        o_all = acc_ref[...] / l_ref[...]
        o = o_all[:N_HEADS] - lam * o_all[N_HEADS:]
        o_ref[...] = _sub_ln(o, g_ref[...], lam_init).astype(o_ref.dtype)


def _decode_attn(q, k_new, v_new, cache_k, cache_v, page_table, lq1, lk1, lq2, lk2, g, *,
                 layer, lam_init, pages_per_step):
    n_seq, n_pages = page_table.shape
    page_rows = cache_k.shape[2]
    n_steps = n_pages // pages_per_step
    small = lambda a: pl.BlockSpec(a.shape, lambda s, p, pt: (0,) * a.ndim)
    per_seq = pl.BlockSpec((None, N_HEADS, HEAD_W), lambda s, p, pt: (s, 0, 0))

    def page_spec(j):
        return pl.BlockSpec((None, None, page_rows, HEAD_W),
                            lambda s, p, pt: (layer, pt[s, p * pages_per_step + j], 0, 0))

    pages = [page_spec(j) for j in range(pages_per_step)]
    return pl.pallas_call(
        functools.partial(_decode_attn_kernel, lam_init=lam_init, pages_per_step=pages_per_step),
        grid_spec=pltpu.PrefetchScalarGridSpec(
            num_scalar_prefetch=1,
            grid=(n_seq, n_steps),
            in_specs=[small(lq1), small(lk1), small(lq2), small(lk2), small(g), per_seq, per_seq, per_seq]
                     + pages + pages,
            out_specs=per_seq,
            scratch_shapes=[pltpu.VMEM((2 * N_HEADS, 1), F32), pltpu.VMEM((2 * N_HEADS, 1), F32),
                            pltpu.VMEM((2 * N_HEADS, HEAD_W), F32)],
        ),
        out_shape=jax.ShapeDtypeStruct((n_seq, N_HEADS, HEAD_W), BF16),
        compiler_params=_compiler_params(("parallel", "arbitrary")),
        name="decode_attn",
    )(page_table, lq1, lk1, lq2, lk2, g, q, k_new, v_new,
      *([cache_k] * pages_per_step), *([cache_v] * pages_per_step))


def _gelu_tanh(x):
    return 0.5 * x * (1.0 + jnp.tanh(math.sqrt(2.0 / math.pi) * (x + 0.044715 * (x * x * x))))


def _lru_coefficients(xc, wa_ref, ba_ref, wx_ref, bx_ref, lam_ref):
    xcb = xc.astype(BF16)
    r = jax.nn.sigmoid(_dot(xcb, wa_ref[...]) + ba_ref[...])
    i = jax.nn.sigmoid(_dot(xcb, wx_ref[...]) + bx_ref[...])
    log_a = -LRU_C * r * jax.nn.softplus(-lam_ref[...])
    a = jnp.exp(log_a)
    b = jnp.sqrt(-jnp.tanh(log_a) * (a * a + 1.0)) * i * xc
    return a, b


def _rglru_seq_kernel(xr_ref, xg_ref, conv0_ref, h0_ref, cw_ref, cb_ref, wa_ref, ba_ref, wx_ref, bx_ref, lam_ref,
                      y_ref, convn_ref, hn_ref, xs_ref, a_ref, b_ref, h_ref, *, tt, nb):
    t = pl.program_id(0)
    d_rec = xr_ref.shape[-1]
    pad = 8
    hist = CONV_W - 1

    @pl.when(t == 0)
    def _():
        xs_ref[:, pad - hist:pad, :] = conv0_ref[...]
        h_ref[...] = h0_ref[...]

    @pl.when(t > 0)
    def _():
        xs_ref[:, pad - hist:pad, :] = xs_ref[:, pad + tt - hist:pad + tt, :]

    xs_ref[:, pad:, :] = xr_ref[...]
    xc = cb_ref[...][None]
    for j in range(CONV_W):
        xc = xc + xs_ref[:, pad - hist + j:pad - hist + j + tt, :] * cw_ref[j:j + 1, :][None]
    xc = xc.reshape(nb * tt, d_rec)

    a, b = _lru_coefficients(xc, wa_ref, ba_ref, wx_ref, bx_ref, lam_ref)
    n_chunks = d_rec // LANES
    for c in range(n_chunks):
        a_ref[c] = a[:, c * LANES:(c + 1) * LANES]
        b_ref[c] = b[:, c * LANES:(c + 1) * LANES]

    def step(s, hs):
        rows = pl.ds(s, nb, stride=tt)
        new = []
        for c in range(n_chunks):
            h = a_ref[c, rows, :] * hs[c] + b_ref[c, rows, :]
            b_ref[c, rows, :] = h
            new.append(h)
        return tuple(new)

    h0 = h_ref[...]
    hs = lax.fori_loop(0, tt, step, tuple(h0[:, c * LANES:(c + 1) * LANES] for c in range(n_chunks)), unroll=8)
    h = jnp.concatenate(hs, axis=-1)
    h_ref[...] = h

    y = jnp.concatenate([b_ref[c] for c in range(n_chunks)], axis=-1) * _gelu_tanh(
        xg_ref[...].reshape(nb * tt, d_rec))
    y_ref[...] = y.reshape(nb, tt, d_rec).astype(y_ref.dtype)
    convn_ref[...] = xs_ref[:, pad + tt - hist:pad + tt, :]
    hn_ref[...] = h


def _rglru_seq(xr, xg, conv0, h0, cw, cb, wa, ba, wx, bx, lam, *, tt):
    nb, seq, d_rec = xr.shape
    tile = pl.BlockSpec((nb, tt, d_rec), lambda t: (0, t, 0))
    whole = lambda a: pl.BlockSpec(a.shape, lambda t: (0,) * a.ndim)
    return pl.pallas_call(
        functools.partial(_rglru_seq_kernel, tt=tt, nb=nb),
        grid=(seq // tt,),
        in_specs=[tile, tile] + [whole(a) for a in (conv0, h0, cw, cb, wa, ba, wx, bx, lam)],
        out_specs=[tile, whole(conv0), whole(h0)],
        out_shape=[jax.ShapeDtypeStruct((nb, seq, d_rec), BF16),
                   jax.ShapeDtypeStruct(conv0.shape, F32), jax.ShapeDtypeStruct(h0.shape, F32)],
        scratch_shapes=[pltpu.VMEM((nb, 8 + tt, d_rec), F32), pltpu.VMEM((d_rec // LANES, nb * tt, LANES), F32),
                        pltpu.VMEM((d_rec // LANES, nb * tt, LANES), F32), pltpu.VMEM((nb, d_rec), F32)],
        compiler_params=_compiler_params(("arbitrary",)),
        name="rglru_seq",
    )(xr, xg, conv0, h0, cw, cb, wa, ba, wx, bx, lam)


def _rglru_step_kernel(xr_ref, xg_ref, conv_ref, h0_ref, cw_ref, cb_ref, wa_ref, ba_ref, wx_ref, bx_ref, lam_ref,
                       y_ref, convn_ref, hn_ref):
    xr = xr_ref[...]
    xc = cb_ref[...] + xr * cw_ref[CONV_W - 1:CONV_W, :]
    for j in range(CONV_W - 1):
        xc = xc + conv_ref[j] * cw_ref[j:j + 1, :]
    a, b = _lru_coefficients(xc, wa_ref, ba_ref, wx_ref, bx_ref, lam_ref)
    h = a * h0_ref[...] + b
    y_ref[...] = (h * _gelu_tanh(xg_ref[...])).astype(y_ref.dtype)
    for j in range(CONV_W - 2):
        convn_ref[j] = conv_ref[j + 1]
    convn_ref[CONV_W - 2] = xr
    hn_ref[...] = h


def _rglru_step(xr, xg, conv_t, h0, cw, cb, wa, ba, wx, bx, lam):
    args = (xr, xg, conv_t, h0, cw, cb, wa, ba, wx, bx, lam)
    whole = lambda a: pl.BlockSpec(a.shape, lambda: (0,) * a.ndim)
    return pl.pallas_call(
        _rglru_step_kernel,
        in_specs=[whole(a) for a in args],
        out_specs=[whole(xr), whole(conv_t), whole(h0)],
        out_shape=[jax.ShapeDtypeStruct(xr.shape, BF16), jax.ShapeDtypeStruct(conv_t.shape, F32),
                   jax.ShapeDtypeStruct(h0.shape, F32)],
        name="rglru_step",
    )(*args)


def _out_ln_kernel(x_ref, attn_ref, rec_ref, w_ref, g_ref, b_ref, o_ref, *, alpha):
    mixed = jnp.concatenate([attn_ref[...], rec_ref[...]], axis=-1)
    o_ref[...] = _layer_norm(alpha * x_ref[...] + _dot(mixed, w_ref[...]), g_ref[...], b_ref[...])


def _out_ln(x, attn, rec, w_out, g, b, *, alpha, tm):
    n, d = x.shape
    row = lambda w: pl.BlockSpec((tm, w), lambda i: (i, 0))
    return pl.pallas_call(
        functools.partial(_out_ln_kernel, alpha=alpha),
        grid=(n // tm,),
        in_specs=[row(d), row(attn.shape[1]), row(rec.shape[1]), _resident((d, d)),
                  _resident((1, d)), _resident((1, d))],
        out_specs=row(d),
        out_shape=jax.ShapeDtypeStruct((n, d), F32),
        compiler_params=_compiler_params(("parallel",)),
        name="out_ln",
    )(x, attn, rec, w_out, g, b)


def _block_diag(w):
    n_blk, blk, _ = w.shape
    eye = jnp.eye(n_blk, dtype=w.dtype)
    return jnp.einsum("nde,nm->ndme", w, eye).reshape(n_blk * blk, n_blk * blk)


def _row_tile(n, preferred):
    return preferred if n % preferred == 0 else n


def kernel(x_prompt, x_sample, cache_k, cache_v, state_conv, state_h, page_table, ln1_g, ln1_b, ln2_g, ln2_b, ln3_g, ln3_b, ffn1_w_gate, ffn1_w_up, ffn1_w_down, ffn2_w_gate, ffn2_w_up, ffn2_w_down, w_in, w_out, attn_lam_q1, attn_lam_k1, attn_lam_q2, attn_lam_k2, attn_subln_g, conv_w, conv_b, gate_a_w, gate_a_b, gate_x_w, gate_x_b, lru_lambda):
    depth = w_in.shape[0]
    batch, seq, d_model = x_prompt.shape
    n_seq = x_sample.shape[0]
    assert x_sample.shape[1] == 1, "the sample group decodes one token per sequence"
    d_attn = N_HEADS * HEAD_W
    d_rec = d_model - d_attn
    alpha = (2 * depth) ** 0.25
    n_pool, page = cache_k.shape[1:3]
    n_tok = batch * seq

    cache_k2 = cache_k.reshape(depth, n_pool, page * N_HEADS, HEAD_W)
    cache_v2 = cache_v.reshape(depth, n_pool, page * N_HEADS, HEAD_W)
    bf = lambda w: w.astype(BF16)
    vec = lambda p, l: p[l][None, :]

    xp = x_prompt.reshape(n_tok, d_model)
    xs = x_sample.reshape(n_seq, d_model)
    conv_p = jnp.zeros((batch, CONV_W - 1, d_rec), F32)
    h_p = jnp.zeros((batch, d_rec), F32)
    kp_all = jnp.zeros((depth, n_tok, N_HEADS, HEAD_W), F32)
    vp_all = jnp.zeros((depth, n_tok, N_HEADS, HEAD_W), F32)
    ks_all = jnp.zeros((depth, n_seq, N_HEADS, HEAD_W), F32)
    vs_all = jnp.zeros((depth, n_seq, N_HEADS, HEAD_W), F32)

    tm = _row_tile(n_tok, 256)
    tm_ffn = _row_tile(n_tok, 512)
    tq = _row_tile(seq, 256)
    tk = _row_tile(seq, 512)
    tt = _row_tile(seq, 128)
    pages_per_step = 8 if page_table.shape[1] % 8 == 0 else 1

    outs = {name: [] for name in ("cp", "hp", "cs", "hs")}
    for l in range(depth):
        lam_init = 0.8 - 0.6 * math.exp(-0.3 * l)
        ffn1 = (bf(ffn1_w_gate[l]), bf(ffn1_w_up[l]), bf(ffn1_w_down[l]), vec(ln1_g, l), vec(ln1_b, l))
        ffn2 = (bf(ffn2_w_gate[l]), bf(ffn2_w_up[l]), bf(ffn2_w_down[l]), vec(ln3_g, l), vec(ln3_b, l))
        w_in_l, w_out_l = bf(w_in[l]), bf(w_out[l])
        lam_vecs = (vec(attn_lam_q1, l), vec(attn_lam_k1, l), vec(attn_lam_q2, l), vec(attn_lam_k2, l))
        sub_g = vec(attn_subln_g, l)
        rec = (conv_w[l], vec(conv_b, l), bf(_block_diag(gate_a_w[l])), vec(gate_a_b, l),
               bf(_block_diag(gate_x_w[l])), vec(gate_x_b, l), vec(lru_lambda, l))

        xp = _ffn_ln(xp, *ffn1, alpha=alpha, tm=tm_ffn)
        q, kp_all, vp_all, kb, vb, xr, xg = _in_proj(xp, w_in_l, kp_all, vp_all, layer=l,
                                                     d_attn=d_attn, d_rec=d_rec, tm=tm)
        attn = _prompt_attn(q, kb, vb, *lam_vecs, sub_g, lam_init=lam_init, batch=batch, seq=seq, tq=tq, tk=tk)
        y, conv_n, h_n = _rglru_seq(xr.reshape(batch, seq, d_rec), xg.reshape(batch, seq, d_rec),
                                    conv_p, h_p, *rec, tt=tt)
        xp = _out_ln(xp, attn, y.reshape(n_tok, d_rec), w_out_l, vec(ln2_g, l), vec(ln2_b, l),
                     alpha=alpha, tm=tm)
        xp = _ffn_ln(xp, *ffn2, alpha=alpha, tm=tm_ffn)
        outs["cp"].append(conv_n)
        outs["hp"].append(h_n)

        xs = _ffn_ln(xs, *ffn1, alpha=alpha, tm=n_seq)
        q, ks_all, vs_all, _, _, xr, xg = _in_proj(xs, w_in_l, ks_all, vs_all, layer=l,
                                                   d_attn=d_attn, d_rec=d_rec, tm=n_seq)
        attn = _decode_attn(q.reshape(n_seq, N_HEADS, HEAD_W), ks_all[l], vs_all[l], cache_k2, cache_v2, page_table,
                            *lam_vecs, sub_g, layer=l, lam_init=lam_init, pages_per_step=pages_per_step)
        y, conv_n, h_n = _rglru_step(xr, xg, jnp.swapaxes(state_conv[l], 0, 1), state_h[l], *rec)
        xs = _out_ln(xs, attn.reshape(n_seq, d_attn), y, w_out_l, vec(ln2_g, l), vec(ln2_b, l),
                     alpha=alpha, tm=n_seq)
        xs = _ffn_ln(xs, *ffn2, alpha=alpha, tm=n_seq)
        outs["cs"].append(jnp.swapaxes(conv_n, 0, 1))
        outs["hs"].append(h_n)

    stack = lambda name: jnp.stack(outs[name])
    return (xp.reshape(batch, seq, d_model), xs.reshape(n_seq, 1, d_model),
            kp_all.reshape(depth, batch, seq, N_HEADS, HEAD_W), vp_all.reshape(depth, batch, seq, N_HEADS, HEAD_W),
            stack("cp"), stack("hp"),
            ks_all.reshape(depth, n_seq, 1, N_HEADS, HEAD_W), vs_all.reshape(depth, n_seq, 1, N_HEADS, HEAD_W),
            stack("cs"), stack("hs"))
```

```python
import functools
import math

import jax
import jax.numpy as jnp
from jax import lax
from jax.experimental import pallas as pl
from jax.experimental.pallas import tpu as pltpu

F32 = jnp.float32
BF16 = jnp.bfloat16

LANES = 128
N_HEADS = 4
HEAD_DIM = 64
HEAD_W = 2 * HEAD_DIM
CONV_W = 4
LRU_C = 8.0
LN_EPS = 1e-5
NEG_INF = -1e30
QK_SCALE = HEAD_DIM ** -0.5

VMEM_LIMIT_BYTES = 56 * 1024 * 1024


def _compiler_params(semantics):
    return pltpu.CompilerParams(dimension_semantics=semantics, vmem_limit_bytes=VMEM_LIMIT_BYTES)


def _resident(shape):
    return pl.BlockSpec(shape, lambda *_: (0,) * len(shape), pipeline_mode=pl.Buffered(1))


def _layer_norm(y, g, b):
    mu = jnp.mean(y, axis=-1, keepdims=True)
    d = y - mu
    var = jnp.mean(d * d, axis=-1, keepdims=True)
    return d * lax.rsqrt(var + LN_EPS) * g + b


def _dot(a, b):
    return jnp.dot(a, b, preferred_element_type=F32)


def _ffn_ln_kernel(x_ref, wg_ref, wu_ref, wd_ref, g_ref, b_ref, o_ref, *, alpha):
    x = x_ref[...]
    xb = x.astype(BF16)
    gate = _dot(xb, wg_ref[...])
    up = _dot(xb, wu_ref[...])
    h = (gate * jax.nn.sigmoid(gate) * up).astype(BF16)
    f = _dot(h, wd_ref[...])
    o_ref[...] = _layer_norm(alpha * x + 0.5 * f, g_ref[...], b_ref[...])


def _ffn_ln(x, wg, wu, wd, g, b, *, alpha, tm):
    n, d = x.shape
    d_ff = wg.shape[1]
    return pl.pallas_call(
        functools.partial(_ffn_ln_kernel, alpha=alpha),
        grid=(n // tm,),
        in_specs=[
            pl.BlockSpec((tm, d), lambda i: (i, 0)),
            _resident((d, d_ff)), _resident((d, d_ff)), _resident((d_ff, d)),
            _resident((1, d)), _resident((1, d)),
        ],
        out_specs=pl.BlockSpec((tm, d), lambda i: (i, 0)),
        out_shape=jax.ShapeDtypeStruct((n, d), F32),
        compiler_params=_compiler_params(("parallel",)),
        name="ffn_ln",
    )(x, wg, wu, wd, g, b)


def _in_proj_kernel(x_ref, w_ref, k_all_in, v_all_in, q_ref, k_all_ref, v_all_ref, kb_ref, vb_ref, xr_ref, xg_ref,
                    *, d_attn, d_rec):
    del k_all_in, v_all_in
    proj = _dot(x_ref[...].astype(BF16), w_ref[...])
    q = proj[:, :d_attn]
    k = proj[:, d_attn:2 * d_attn]
    v = proj[:, 2 * d_attn:3 * d_attn]
    q_ref[...] = (q * QK_SCALE).astype(BF16)
    for h in range(N_HEADS):
        k_all_ref[:, h, :] = k[:, h * HEAD_W:(h + 1) * HEAD_W]
        v_all_ref[:, h, :] = v[:, h * HEAD_W:(h + 1) * HEAD_W]
    kb_ref[...] = k.astype(BF16)
    vb_ref[...] = v.astype(BF16)
    xr_ref[...] = proj[:, 3 * d_attn:3 * d_attn + d_rec]
    xg_ref[...] = proj[:, 3 * d_attn + d_rec:]


def _in_proj(x, w_in, k_all, v_all, *, layer, d_attn, d_rec, tm):
    n, d = x.shape
    d_in = w_in.shape[1]
    row = lambda w: pl.BlockSpec((tm, w), lambda i: (i, 0))
    sds = lambda w, dt: jax.ShapeDtypeStruct((n, w), dt)
    layer_rows = pl.BlockSpec((None, tm, N_HEADS, HEAD_W), lambda i: (layer, i, 0, 0))
    in_hbm = pl.BlockSpec(memory_space=pl.ANY)
    return pl.pallas_call(
        functools.partial(_in_proj_kernel, d_attn=d_attn, d_rec=d_rec),
        grid=(n // tm,),
        in_specs=[row(d), _resident((d, d_in)), in_hbm, in_hbm],
        out_specs=[row(d_attn), layer_rows, layer_rows, row(d_attn), row(d_attn), row(d_rec), row(d_rec)],
        out_shape=[sds(d_attn, BF16), jax.ShapeDtypeStruct(k_all.shape, F32), jax.ShapeDtypeStruct(v_all.shape, F32),
                   sds(d_attn, BF16), sds(d_attn, BF16), sds(d_rec, F32), sds(d_rec, F32)],
        input_output_aliases={2: 1, 3: 2},
        compiler_params=_compiler_params(("parallel",)),
        name="in_proj",
    )(x, w_in, k_all, v_all)


def _lambda(lq1_ref, lk1_ref, lq2_ref, lk2_ref, lam_init):
    s1 = jnp.sum(lq1_ref[...] * lk1_ref[...], axis=-1, keepdims=True)
    s2 = jnp.sum(lq2_ref[...] * lk2_ref[...], axis=-1, keepdims=True)
    return jnp.exp(s1) - jnp.exp(s2) + lam_init


def _sub_ln(o, g, lam_init):
    ms = jnp.mean(o * o, axis=-1, keepdims=True)
    return o * lax.rsqrt(ms + LN_EPS) * g * (1.0 - lam_init)


def _stack_maps(q):
    lane = lax.broadcasted_iota(jnp.int32, q.shape, 1)
    zero = jnp.zeros_like(q)
    return jnp.concatenate([jnp.where(lane < HEAD_DIM, q, zero), jnp.where(lane >= HEAD_DIM, q, zero)], axis=0)


def _prompt_attn_kernel(lq1_ref, lk1_ref, lq2_ref, lk2_ref, g_ref, q_ref, k_ref, v_ref, o_ref, vx_ref,
                        *, lam_init, tq, nq, n_heads):
    qi = pl.program_id(2)

    @pl.when(qi == 0)
    def _():
        for h in range(n_heads):
            vx_ref[h, :, :HEAD_W] = v_ref[:, h * HEAD_W:(h + 1) * HEAD_W]
            vx_ref[h, :, HEAD_W:] = jnp.ones((vx_ref.shape[1], HEAD_W), BF16)

    lam = _lambda(lq1_ref, lk1_ref, lq2_ref, lk2_ref, lam_init)
    row = lax.broadcasted_iota(jnp.int32, (2 * tq, tq), 0)
    col = lax.broadcasted_iota(jnp.int32, (2 * tq, tq), 1)
    causal = col <= jnp.where(row >= tq, row - tq, row)

    def attend_head(h, n_keys):
        lanes = slice(h * HEAD_W, (h + 1) * HEAD_W)
        qs = _stack_maps(q_ref[:, lanes])
        s = lax.dot_general(qs, k_ref[:n_keys, lanes], (((1,), (1,)), ((), ())), preferred_element_type=F32)
        s_diag = jnp.where(causal, s[:, n_keys - tq:], NEG_INF)
        chunks = [s[:, c * LANES:(c + 1) * LANES] for c in range((n_keys - tq) // LANES)]
        chunks += [s_diag[:, c * LANES:(c + 1) * LANES] for c in range(tq // LANES)]
        m = jnp.max(functools.reduce(jnp.maximum, chunks), axis=-1, keepdims=True)
        m_b = jnp.broadcast_to(m, (2 * tq, LANES))
        p = jnp.concatenate([jnp.exp(c - m_b) for c in chunks], axis=-1).astype(BF16)
        acc = _dot(p, vx_ref[h, :n_keys, :])
        o_all = acc[:, :HEAD_W] / acc[:, HEAD_W:]
        o = o_all[:tq] - lam * o_all[tq:]
        return _sub_ln(o, g_ref[...], lam_init).astype(o_ref.dtype)

    def attend(n_keys):
        o_ref[...] = jnp.concatenate([attend_head(h, n_keys) for h in range(n_heads)], axis=-1)

    for i in range(nq):
        pl.when(qi == i)(functools.partial(attend, (i + 1) * tq))


def _prompt_attn(q, kb, vb, lq1, lk1, lq2, lk2, g, *, lam_init, batch, seq, tq, heads_per_step):
    n, d_attn = q.shape
    nq = seq // tq
    width = heads_per_step * HEAD_W
    small = lambda a: _resident(a.shape)
    return pl.pallas_call(
        functools.partial(_prompt_attn_kernel, lam_init=lam_init, tq=tq, nq=nq, n_heads=heads_per_step),
        grid=(batch, N_HEADS // heads_per_step, nq),
        in_specs=[
            small(lq1), small(lk1), small(lq2), small(lk2), small(g),
            pl.BlockSpec((tq, width), lambda b, h, i: (b * nq + i, h)),
            pl.BlockSpec((seq, width), lambda b, h, i: (b, h)),
            pl.BlockSpec((seq, width), lambda b, h, i: (b, h)),
        ],
        out_specs=pl.BlockSpec((tq, width), lambda b, h, i: (b * nq + i, h)),
        out_shape=jax.ShapeDtypeStruct((n, d_attn), BF16),
        scratch_shapes=[pltpu.VMEM((heads_per_step, seq, 2 * HEAD_W), BF16)],
        compiler_params=_compiler_params(("parallel", "parallel", "arbitrary")),
        name="prompt_attn",
    )(lq1, lk1, lq2, lk2, g, q, kb, vb)


def _decode_attn_kernel(pt_ref, lq1_ref, lk1_ref, lq2_ref, lk2_ref, g_ref, q_ref, kn_ref, vn_ref, *rest,
                        lam_init, pages_per_step, pages_per_group):
    del pt_ref
    k_refs = rest[:pages_per_step]
    v_refs = rest[pages_per_step:2 * pages_per_step]
    o_ref, m_ref, l_ref, acc_ref = rest[2 * pages_per_step:]
    step = pl.program_id(1)

    qm_f32 = _stack_maps(q_ref[...].astype(F32))
    qm = qm_f32.astype(BF16)

    @pl.when(step == 0)
    def _():
        m_ref[...] = jnp.full(m_ref.shape, NEG_INF, F32)
        l_ref[...] = jnp.zeros(l_ref.shape, F32)
        acc_ref[...] = jnp.zeros(acc_ref.shape, F32)

    def merge(parts):
        m_prev = m_ref[...]
        m_new = functools.reduce(jnp.maximum, [m for m, _, _ in parts], m_prev)
        corr = jnp.exp(m_prev - m_new)
        l_new = corr * l_ref[...]
        acc_new = corr * acc_ref[...]
        for m, l, acc in parts:
            w = jnp.exp(m - m_new)
            l_new = l_new + w * l
            acc_new = acc_new + w * acc
        m_ref[...] = m_new
        l_ref[...] = l_new
        acc_ref[...] = acc_new

    parts = []
    for g0 in range(0, pages_per_step, pages_per_group):
        group = slice(g0, g0 + pages_per_group)
        k_cat = jnp.concatenate([r[...].astype(BF16) for r in k_refs[group]], axis=0)
        v_cat = jnp.concatenate([r[...].astype(BF16) for r in v_refs[group]], axis=0)
        s = lax.dot_general(qm, k_cat, (((1,), (1,)), ((), ())), preferred_element_type=F32)
        row = lax.broadcasted_iota(jnp.int32, s.shape, 0)
        col = lax.broadcasted_iota(jnp.int32, s.shape, 1)
        s = jnp.where(col % N_HEADS == row % N_HEADS, s, NEG_INF)
        m = jnp.max(s, axis=-1, keepdims=True)
        p = jnp.exp(s - m)
        parts.append((m, jnp.sum(p, axis=-1, keepdims=True), _dot(p.astype(BF16), v_cat)))
    merge(parts)

    @pl.when(step == pl.num_programs(1) - 1)
    def _():
        kn = kn_ref[...].astype(BF16).astype(F32)
        vn = vn_ref[...].astype(BF16).astype(F32)
        s_self = jnp.sum(qm_f32 * jnp.concatenate([kn, kn], axis=0), axis=-1, keepdims=True)
        merge([(s_self, jnp.ones_like(s_self), jnp.concatenate([vn, vn], axis=0))])

        lam = _lambda(lq1_ref, lk1_ref, lq2_ref, lk2_ref, lam_init)
        o_all = acc_ref[...] / l_ref[...]
        o = o_all[:N_HEADS] - lam * o_all[N_HEADS:]
        o_ref[...] = _sub_ln(o, g_ref[...], lam_init).astype(o_ref.dtype)


def _decode_attn(q, k_new, v_new, cache_k, cache_v, page_table, lq1, lk1, lq2, lk2, g, *,
                 layer, lam_init, pages_per_step, pages_per_group):
    n_seq, n_pages = page_table.shape
    page_rows = cache_k.shape[2]
    n_steps = n_pages // pages_per_step
    small = lambda a: pl.BlockSpec(a.shape, lambda s, p, pt: (0,) * a.ndim)
    per_seq = pl.BlockSpec((None, N_HEADS, HEAD_W), lambda s, p, pt: (s, 0, 0))

    def page_spec(j):
        return pl.BlockSpec((None, None, page_rows, HEAD_W),
                            lambda s, p, pt: (layer, pt[s, p * pages_per_step + j], 0, 0))

    pages = [page_spec(j) for j in range(pages_per_step)]
    return pl.pallas_call(
        functools.partial(_decode_attn_kernel, lam_init=lam_init, pages_per_step=pages_per_step,
                          pages_per_group=pages_per_group),
        grid_spec=pltpu.PrefetchScalarGridSpec(
            num_scalar_prefetch=1,
            grid=(n_seq, n_steps),
            in_specs=[small(lq1), small(lk1), small(lq2), small(lk2), small(g), per_seq, per_seq, per_seq]
                     + pages + pages,
            out_specs=per_seq,
            scratch_shapes=[pltpu.VMEM((2 * N_HEADS, 1), F32), pltpu.VMEM((2 * N_HEADS, 1), F32),
                            pltpu.VMEM((2 * N_HEADS, HEAD_W), F32)],
        ),
        out_shape=jax.ShapeDtypeStruct((n_seq, N_HEADS, HEAD_W), BF16),
        compiler_params=_compiler_params(("parallel", "arbitrary")),
        name="decode_attn",
    )(page_table, lq1, lk1, lq2, lk2, g, q, k_new, v_new,
      *([cache_k] * pages_per_step), *([cache_v] * pages_per_step))


def _gelu_tanh(x):
    return 0.5 * x * (1.0 + jnp.tanh(math.sqrt(2.0 / math.pi) * (x + 0.044715 * (x * x * x))))


def _lru_coefficients(xc, wa_ref, ba_ref, wx_ref, bx_ref, lam_ref):
    xcb = xc.astype(BF16)
    r = jax.nn.sigmoid(_dot(xcb, wa_ref[...]) + ba_ref[...])
    i = jax.nn.sigmoid(_dot(xcb, wx_ref[...]) + bx_ref[...])
    log_a = -LRU_C * r * jax.nn.softplus(-lam_ref[...])
    a = jnp.exp(log_a)
    b = jnp.sqrt(-jnp.tanh(log_a) * (a * a + 1.0)) * i * xc
    return a, b


def _rglru_seq_kernel(xr_ref, xg_ref, conv0_ref, h0_ref, cw_ref, cb_ref, wa_ref, ba_ref, wx_ref, bx_ref, lam_ref,
                      y_ref, convn_ref, hn_ref, xs_ref, a_ref, b_ref, h_ref, *, tt, nb):
    t = pl.program_id(0)
    d_rec = xr_ref.shape[-1]
    pad = 8
    hist = CONV_W - 1

    @pl.when(t == 0)
    def _():
        xs_ref[:, pad - hist:pad, :] = conv0_ref[...]
        h_ref[...] = h0_ref[...]

    @pl.when(t > 0)
    def _():
        xs_ref[:, pad - hist:pad, :] = xs_ref[:, pad + tt - hist:pad + tt, :]

    xs_ref[:, pad:, :] = xr_ref[...]
    xc = cb_ref[...][None]
    for j in range(CONV_W):
        xc = xc + xs_ref[:, pad - hist + j:pad - hist + j + tt, :] * cw_ref[j:j + 1, :][None]
    xc = xc.reshape(nb * tt, d_rec)

    a, b = _lru_coefficients(xc, wa_ref, ba_ref, wx_ref, bx_ref, lam_ref)
    n_chunks = d_rec // LANES
    for c in range(n_chunks):
        a_ref[c] = a[:, c * LANES:(c + 1) * LANES]
        b_ref[c] = b[:, c * LANES:(c + 1) * LANES]

    def step(s, hs):
        rows = pl.ds(s, nb, stride=tt)
        new = []
        for c in range(n_chunks):
            h = a_ref[c, rows, :] * hs[c] + b_ref[c, rows, :]
            b_ref[c, rows, :] = h
            new.append(h)
        return tuple(new)

    h0 = h_ref[...]
    hs = lax.fori_loop(0, tt, step, tuple(h0[:, c * LANES:(c + 1) * LANES] for c in range(n_chunks)), unroll=8)
    h = jnp.concatenate(hs, axis=-1)
    h_ref[...] = h

    y = jnp.concatenate([b_ref[c] for c in range(n_chunks)], axis=-1) * _gelu_tanh(
        xg_ref[...].reshape(nb * tt, d_rec))
    y_ref[...] = y.reshape(nb, tt, d_rec).astype(y_ref.dtype)
    convn_ref[...] = xs_ref[:, pad + tt - hist:pad + tt, :]
    hn_ref[...] = h


def _rglru_seq(xr, xg, conv0, h0, cw, cb, wa, ba, wx, bx, lam, *, tt):
    nb, seq, d_rec = xr.shape
    tile = pl.BlockSpec((nb, tt, d_rec), lambda t: (0, t, 0))
    whole = lambda a: pl.BlockSpec(a.shape, lambda t: (0,) * a.ndim)
    return pl.pallas_call(
        functools.partial(_rglru_seq_kernel, tt=tt, nb=nb),
        grid=(seq // tt,),
        in_specs=[tile, tile] + [whole(a) for a in (conv0, h0, cw, cb, wa, ba, wx, bx, lam)],
        out_specs=[tile, whole(conv0), whole(h0)],
        out_shape=[jax.ShapeDtypeStruct((nb, seq, d_rec), BF16),
                   jax.ShapeDtypeStruct(conv0.shape, F32), jax.ShapeDtypeStruct(h0.shape, F32)],
        scratch_shapes=[pltpu.VMEM((nb, 8 + tt, d_rec), F32), pltpu.VMEM((d_rec // LANES, nb * tt, LANES), F32),
                        pltpu.VMEM((d_rec // LANES, nb * tt, LANES), F32), pltpu.VMEM((nb, d_rec), F32)],
        compiler_params=_compiler_params(("arbitrary",)),
        name="rglru_seq",
    )(xr, xg, conv0, h0, cw, cb, wa, ba, wx, bx, lam)


def _rglru_step_kernel(xr_ref, xg_ref, conv_ref, h0_ref, cw_ref, cb_ref, wa_ref, ba_ref, wx_ref, bx_ref, lam_ref,
                       y_ref, convn_ref, hn_ref):
    xr = xr_ref[...]
    xc = cb_ref[...] + xr * cw_ref[CONV_W - 1:CONV_W, :]
    for j in range(CONV_W - 1):
        xc = xc + conv_ref[j] * cw_ref[j:j + 1, :]
    a, b = _lru_coefficients(xc, wa_ref, ba_ref, wx_ref, bx_ref, lam_ref)
    h = a * h0_ref[...] + b
    y_ref[...] = (h * _gelu_tanh(xg_ref[...])).astype(y_ref.dtype)
    for j in range(CONV_W - 2):
        convn_ref[j] = conv_ref[j + 1]
    convn_ref[CONV_W - 2] = xr
    hn_ref[...] = h


def _rglru_step(xr, xg, conv_t, h0, cw, cb, wa, ba, wx, bx, lam):
    args = (xr, xg, conv_t, h0, cw, cb, wa, ba, wx, bx, lam)
    whole = lambda a: pl.BlockSpec(a.shape, lambda: (0,) * a.ndim)
    return pl.pallas_call(
        _rglru_step_kernel,
        in_specs=[whole(a) for a in args],
        out_specs=[whole(xr), whole(conv_t), whole(h0)],
        out_shape=[jax.ShapeDtypeStruct(xr.shape, BF16), jax.ShapeDtypeStruct(conv_t.shape, F32),
                   jax.ShapeDtypeStruct(h0.shape, F32)],
        name="rglru_step",
    )(*args)


def _out_ln_kernel(x_ref, attn_ref, rec_ref, w_ref, g_ref, b_ref, o_ref, *, alpha):
    mixed = jnp.concatenate([attn_ref[...], rec_ref[...]], axis=-1)
    o_ref[...] = _layer_norm(alpha * x_ref[...] + _dot(mixed, w_ref[...]), g_ref[...], b_ref[...])


def _out_ln(x, attn, rec, w_out, g, b, *, alpha, tm):
    n, d = x.shape
    row = lambda w: pl.BlockSpec((tm, w), lambda i: (i, 0))
    return pl.pallas_call(
        functools.partial(_out_ln_kernel, alpha=alpha),
        grid=(n // tm,),
        in_specs=[row(d), row(attn.shape[1]), row(rec.shape[1]), _resident((d, d)),
                  _resident((1, d)), _resident((1, d))],
        out_specs=row(d),
        out_shape=jax.ShapeDtypeStruct((n, d), F32),
        compiler_params=_compiler_params(("parallel",)),
        name="out_ln",
    )(x, attn, rec, w_out, g, b)


def _block_diag(w):
    n_blk, blk, _ = w.shape
    eye = jnp.eye(n_blk, dtype=w.dtype)
    return jnp.einsum("nde,nm->ndme", w, eye).reshape(n_blk * blk, n_blk * blk)


def _row_tile(n, preferred):
    return preferred if n % preferred == 0 else n


def kernel(x_prompt, x_sample, cache_k, cache_v, state_conv, state_h, page_table, ln1_g, ln1_b, ln2_g, ln2_b, ln3_g, ln3_b, ffn1_w_gate, ffn1_w_up, ffn1_w_down, ffn2_w_gate, ffn2_w_up, ffn2_w_down, w_in, w_out, attn_lam_q1, attn_lam_k1, attn_lam_q2, attn_lam_k2, attn_subln_g, conv_w, conv_b, gate_a_w, gate_a_b, gate_x_w, gate_x_b, lru_lambda):
    depth = w_in.shape[0]
    batch, seq, d_model = x_prompt.shape
    n_seq = x_sample.shape[0]
    assert x_sample.shape[1] == 1, "the sample group decodes one token per sequence"
    d_attn = N_HEADS * HEAD_W
    d_rec = d_model - d_attn
    alpha = (2 * depth) ** 0.25
    n_pool, page = cache_k.shape[1:3]
    n_tok = batch * seq

    cache_k2 = cache_k.reshape(depth, n_pool, page * N_HEADS, HEAD_W)
    cache_v2 = cache_v.reshape(depth, n_pool, page * N_HEADS, HEAD_W)
    bf = lambda w: w.astype(BF16)
    vec = lambda p, l: p[l][None, :]

    xp = x_prompt.reshape(n_tok, d_model)
    xs = x_sample.reshape(n_seq, d_model)
    conv_p = jnp.zeros((batch, CONV_W - 1, d_rec), F32)
    h_p = jnp.zeros((batch, d_rec), F32)
    kp_all = jnp.zeros((depth, n_tok, N_HEADS, HEAD_W), F32)
    vp_all = jnp.zeros((depth, n_tok, N_HEADS, HEAD_W), F32)
    ks_all = jnp.zeros((depth, n_seq, N_HEADS, HEAD_W), F32)
    vs_all = jnp.zeros((depth, n_seq, N_HEADS, HEAD_W), F32)

    tm = _row_tile(n_tok, 256)
    tm_ffn = _row_tile(n_tok, 512)
    tq = _row_tile(seq, 256)
    tt = _row_tile(seq, 128)
    pages_per_step = next(p for p in (16, 8, 4, 2, 1) if page_table.shape[1] % p == 0)
    pages_per_group = min(4, pages_per_step)

    outs = {name: [] for name in ("cp", "hp", "cs", "hs")}
    for l in range(depth):
        lam_init = 0.8 - 0.6 * math.exp(-0.3 * l)
        ffn1 = (bf(ffn1_w_gate[l]), bf(ffn1_w_up[l]), bf(ffn1_w_down[l]), vec(ln1_g, l), vec(ln1_b, l))
        ffn2 = (bf(ffn2_w_gate[l]), bf(ffn2_w_up[l]), bf(ffn2_w_down[l]), vec(ln3_g, l), vec(ln3_b, l))
        w_in_l, w_out_l = bf(w_in[l]), bf(w_out[l])
        lam_vecs = (vec(attn_lam_q1, l), vec(attn_lam_k1, l), vec(attn_lam_q2, l), vec(attn_lam_k2, l))
        sub_g = vec(attn_subln_g, l)
        rec = (conv_w[l], vec(conv_b, l), bf(_block_diag(gate_a_w[l])), vec(gate_a_b, l),
               bf(_block_diag(gate_x_w[l])), vec(gate_x_b, l), vec(lru_lambda, l))

        xp = _ffn_ln(xp, *ffn1, alpha=alpha, tm=tm_ffn)
        q, kp_all, vp_all, kb, vb, xr, xg = _in_proj(xp, w_in_l, kp_all, vp_all, layer=l,
                                                     d_attn=d_attn, d_rec=d_rec, tm=tm)
        attn = _prompt_attn(q, kb, vb, *lam_vecs, sub_g, lam_init=lam_init, batch=batch, seq=seq, tq=tq,
                            heads_per_step=N_HEADS)
        y, conv_n, h_n = _rglru_seq(xr.reshape(batch, seq, d_rec), xg.reshape(batch, seq, d_rec),
                                    conv_p, h_p, *rec, tt=tt)
        xp = _out_ln(xp, attn, y.reshape(n_tok, d_rec), w_out_l, vec(ln2_g, l), vec(ln2_b, l),
                     alpha=alpha, tm=tm)
        xp = _ffn_ln(xp, *ffn2, alpha=alpha, tm=tm_ffn)
        outs["cp"].append(conv_n)
        outs["hp"].append(h_n)

        xs = _ffn_ln(xs, *ffn1, alpha=alpha, tm=n_seq)
        q, ks_all, vs_all, _, _, xr, xg = _in_proj(xs, w_in_l, ks_all, vs_all, layer=l,
                                                   d_attn=d_attn, d_rec=d_rec, tm=n_seq)
        attn = _decode_attn(q.reshape(n_seq, N_HEADS, HEAD_W), ks_all[l], vs_all[l], cache_k2, cache_v2, page_table,
                            *lam_vecs, sub_g, layer=l, lam_init=lam_init, pages_per_step=pages_per_step,
                            pages_per_group=pages_per_group)
        y, conv_n, h_n = _rglru_step(xr, xg, jnp.swapaxes(state_conv[l], 0, 1), state_h[l], *rec)
        xs = _out_ln(xs, attn.reshape(n_seq, d_attn), y, w_out_l, vec(ln2_g, l), vec(ln2_b, l),
                     alpha=alpha, tm=n_seq)
        xs = _ffn_ln(xs, *ffn2, alpha=alpha, tm=n_seq)
        outs["cs"].append(jnp.swapaxes(conv_n, 0, 1))
        outs["hs"].append(h_n)

    stack = lambda name: jnp.stack(outs[name])
    return (xp.reshape(batch, seq, d_model), xs.reshape(n_seq, 1, d_model),
            kp_all.reshape(depth, batch, seq, N_HEADS, HEAD_W), vp_all.reshape(depth, batch, seq, N_HEADS, HEAD_W),
            stack("cp"), stack("hp"),
            ks_all.reshape(depth, n_seq, 1, N_HEADS, HEAD_W), vs_all.reshape(depth, n_seq, 1, N_HEADS, HEAD_W),
            stack("cs"), stack("hs"))
```

```python
import functools
import math

import jax
import jax.numpy as jnp
from jax import lax
from jax.experimental import pallas as pl
from jax.experimental.pallas import tpu as pltpu

F32 = jnp.float32
BF16 = jnp.bfloat16

LANES = 128
N_HEADS = 4
HEAD_DIM = 64
HEAD_W = 2 * HEAD_DIM
CONV_W = 4
LRU_C = 8.0
LN_EPS = 1e-5
NEG_INF = -1e30
QK_SCALE = HEAD_DIM ** -0.5

VMEM_LIMIT_BYTES = 56 * 1024 * 1024


def _compiler_params(semantics):
    return pltpu.CompilerParams(dimension_semantics=semantics, vmem_limit_bytes=VMEM_LIMIT_BYTES)


def _resident(shape):
    return pl.BlockSpec(shape, lambda *_: (0,) * len(shape), pipeline_mode=pl.Buffered(1))


def _layer_weight(shape, layer):
    return pl.BlockSpec((None,) + shape, lambda *_: (layer,) + (0,) * len(shape), pipeline_mode=pl.Buffered(1))


def _layer_norm(y, g, b):
    mu = jnp.mean(y, axis=-1, keepdims=True)
    d = y - mu
    var = jnp.mean(d * d, axis=-1, keepdims=True)
    return d * lax.rsqrt(var + LN_EPS) * g + b


def _dot(a, b):
    return jnp.dot(a, b, preferred_element_type=F32)


def _swiglu_ln(x, wg_ref, wu_ref, wd_ref, g_ref, b_ref, alpha):
    xb = x.astype(BF16)
    gate = _dot(xb, wg_ref[...])
    up = _dot(xb, wu_ref[...])
    h = (gate * jax.nn.sigmoid(gate) * up).astype(BF16)
    return _layer_norm(alpha * x + 0.5 * _dot(h, wd_ref[...]), g_ref[...], b_ref[...])


def _ffn_proj_kernel(x_ref, wg_ref, wu_ref, wd_ref, g_ref, b_ref, w_in_ref, k_all_in, v_all_in,
                     x1_ref, q_ref, k_all_ref, v_all_ref, kb_ref, vb_ref, xr_ref, xg_ref, *, alpha, d_attn, d_rec):
    del k_all_in, v_all_in
    x1 = _swiglu_ln(x_ref[...], wg_ref, wu_ref, wd_ref, g_ref, b_ref, alpha)
    x1_ref[...] = x1
    proj = _dot(x1.astype(BF16), w_in_ref[...])
    q = proj[:, :d_attn]
    k = proj[:, d_attn:2 * d_attn]
    v = proj[:, 2 * d_attn:3 * d_attn]
    q_ref[...] = (q * QK_SCALE).astype(BF16)
    for h in range(N_HEADS):
        k_all_ref[:, h, :] = k[:, h * HEAD_W:(h + 1) * HEAD_W]
        v_all_ref[:, h, :] = v[:, h * HEAD_W:(h + 1) * HEAD_W]
    kb_ref[...] = k.astype(BF16)
    vb_ref[...] = v.astype(BF16)
    xr_ref[...] = proj[:, 3 * d_attn:3 * d_attn + d_rec]
    xg_ref[...] = proj[:, 3 * d_attn + d_rec:]


def _ffn_proj(x, wg, wu, wd, g, b, w_in, k_all, v_all, *, layer, alpha, d_attn, d_rec, tm):
    n, d = x.shape
    d_ff = wg.shape[-1]
    d_in = w_in.shape[-1]
    weight = functools.partial(_layer_weight, layer=layer)
    row = lambda w: pl.BlockSpec((tm, w), lambda i: (i, 0))
    sds = lambda w, dt: jax.ShapeDtypeStruct((n, w), dt)
    layer_rows = pl.BlockSpec((None, tm, N_HEADS, HEAD_W), lambda i: (layer, i, 0, 0))
    in_hbm = pl.BlockSpec(memory_space=pl.ANY)
    return pl.pallas_call(
        functools.partial(_ffn_proj_kernel, alpha=alpha, d_attn=d_attn, d_rec=d_rec),
        grid=(n // tm,),
        in_specs=[row(d), weight((d, d_ff)), weight((d, d_ff)), weight((d_ff, d)),
                  _resident((1, d)), _resident((1, d)), weight((d, d_in)), in_hbm, in_hbm],
        out_specs=[row(d), row(d_attn), layer_rows, layer_rows, row(d_attn), row(d_attn), row(d_rec), row(d_rec)],
        out_shape=[sds(d, F32), sds(d_attn, BF16), jax.ShapeDtypeStruct(k_all.shape, F32),
                   jax.ShapeDtypeStruct(v_all.shape, F32), sds(d_attn, BF16), sds(d_attn, BF16),
                   sds(d_rec, F32), sds(d_rec, F32)],
        input_output_aliases={7: 2, 8: 3},
        compiler_params=_compiler_params(("parallel",)),
        name="ffn_proj",
    )(x, wg, wu, wd, g, b, w_in, k_all, v_all)


def _mix_ffn_kernel(x_ref, attn_ref, rec_ref, w_out_ref, g2_ref, b2_ref, wg_ref, wu_ref, wd_ref, g3_ref, b3_ref,
                    o_ref, *, alpha):
    mixed = jnp.concatenate([attn_ref[...], rec_ref[...]], axis=-1)
    x2 = _layer_norm(alpha * x_ref[...] + _dot(mixed, w_out_ref[...]), g2_ref[...], b2_ref[...])
    o_ref[...] = _swiglu_ln(x2, wg_ref, wu_ref, wd_ref, g3_ref, b3_ref, alpha)


def _mix_ffn(x, attn, rec, w_out, g2, b2, wg, wu, wd, g3, b3, *, layer, alpha, tm):
    n, d = x.shape
    d_ff = wg.shape[-1]
    weight = functools.partial(_layer_weight, layer=layer)
    row = lambda w: pl.BlockSpec((tm, w), lambda i: (i, 0))
    vec_spec = _resident((1, d))
    return pl.pallas_call(
        functools.partial(_mix_ffn_kernel, alpha=alpha),
        grid=(n // tm,),
        in_specs=[row(d), row(attn.shape[1]), row(rec.shape[1]), weight((d, d)), vec_spec, vec_spec,
                  weight((d, d_ff)), weight((d, d_ff)), weight((d_ff, d)), vec_spec, vec_spec],
        out_specs=row(d),
        out_shape=jax.ShapeDtypeStruct((n, d), F32),
        compiler_params=_compiler_params(("parallel",)),
        name="mix_ffn",
    )(x, attn, rec, w_out, g2, b2, wg, wu, wd, g3, b3)


def _lambda(lq1_ref, lk1_ref, lq2_ref, lk2_ref, lam_init):
    s1 = jnp.sum(lq1_ref[...] * lk1_ref[...], axis=-1, keepdims=True)
    s2 = jnp.sum(lq2_ref[...] * lk2_ref[...], axis=-1, keepdims=True)
    return jnp.exp(s1) - jnp.exp(s2) + lam_init


def _sub_ln(o, g, lam_init):
    ms = jnp.mean(o * o, axis=-1, keepdims=True)
    return o * lax.rsqrt(ms + LN_EPS) * g * (1.0 - lam_init)


def _stack_maps(q):
    lane = lax.broadcasted_iota(jnp.int32, q.shape, 1)
    zero = jnp.zeros_like(q)
    return jnp.concatenate([jnp.where(lane < HEAD_DIM, q, zero), jnp.where(lane >= HEAD_DIM, q, zero)], axis=0)


def _prompt_attn_kernel(lq1_ref, lk1_ref, lq2_ref, lk2_ref, g_ref, q_ref, k_ref, v_ref, o_ref, vx_ref,
                        *, lam_init, tq, nq, n_heads):
    qi = pl.program_id(2)

    @pl.when(qi == 0)
    def _():
        for h in range(n_heads):
            vx_ref[h, :, :HEAD_W] = v_ref[:, h * HEAD_W:(h + 1) * HEAD_W]
            vx_ref[h, :, HEAD_W:] = jnp.ones((vx_ref.shape[1], HEAD_W), BF16)

    lam = _lambda(lq1_ref, lk1_ref, lq2_ref, lk2_ref, lam_init)
    row = lax.broadcasted_iota(jnp.int32, (2 * tq, tq), 0)
    col = lax.broadcasted_iota(jnp.int32, (2 * tq, tq), 1)
    causal = col <= jnp.where(row >= tq, row - tq, row)

    def attend_head(h, n_keys):
        lanes = slice(h * HEAD_W, (h + 1) * HEAD_W)
        qs = _stack_maps(q_ref[:, lanes])
        s = lax.dot_general(qs, k_ref[:n_keys, lanes], (((1,), (1,)), ((), ())), preferred_element_type=F32)
        s_diag = jnp.where(causal, s[:, n_keys - tq:], NEG_INF)
        chunks = [s[:, c * LANES:(c + 1) * LANES] for c in range((n_keys - tq) // LANES)]
        chunks += [s_diag[:, c * LANES:(c + 1) * LANES] for c in range(tq // LANES)]
        m = jnp.max(functools.reduce(jnp.maximum, chunks), axis=-1, keepdims=True)
        m_b = jnp.broadcast_to(m, (2 * tq, LANES))
        p = jnp.concatenate([jnp.exp(c - m_b) for c in chunks], axis=-1).astype(BF16)
        acc = _dot(p, vx_ref[h, :n_keys, :])
        o_all = acc[:, :HEAD_W] / acc[:, HEAD_W:]
        o = o_all[:tq] - lam * o_all[tq:]
        return _sub_ln(o, g_ref[...], lam_init).astype(o_ref.dtype)

    def attend(n_keys):
        o_ref[...] = jnp.concatenate([attend_head(h, n_keys) for h in range(n_heads)], axis=-1)

    for i in range(nq):
        pl.when(qi == i)(functools.partial(attend, (i + 1) * tq))


def _prompt_attn(q, kb, vb, lq1, lk1, lq2, lk2, g, *, lam_init, batch, seq, tq, heads_per_step):
    n, d_attn = q.shape
    nq = seq // tq
    width = heads_per_step * HEAD_W
    small = lambda a: _resident(a.shape)
    return pl.pallas_call(
        functools.partial(_prompt_attn_kernel, lam_init=lam_init, tq=tq, nq=nq, n_heads=heads_per_step),
        grid=(batch, N_HEADS // heads_per_step, nq),
        in_specs=[
            small(lq1), small(lk1), small(lq2), small(lk2), small(g),
            pl.BlockSpec((tq, width), lambda b, h, i: (b * nq + i, h)),
            pl.BlockSpec((seq, width), lambda b, h, i: (b, h)),
            pl.BlockSpec((seq, width), lambda b, h, i: (b, h)),
        ],
        out_specs=pl.BlockSpec((tq, width), lambda b, h, i: (b * nq + i, h)),
        out_shape=jax.ShapeDtypeStruct((n, d_attn), BF16),
        scratch_shapes=[pltpu.VMEM((heads_per_step, seq, 2 * HEAD_W), BF16)],
        compiler_params=_compiler_params(("parallel", "parallel", "arbitrary")),
        name="prompt_attn",
    )(lq1, lk1, lq2, lk2, g, q, kb, vb)


def _decode_attn_kernel(pt_ref, lq1_ref, lk1_ref, lq2_ref, lk2_ref, g_ref, q_ref, kn_ref, vn_ref, *rest,
                        lam_init, pages_per_step, pages_per_group):
    del pt_ref
    k_refs = rest[:pages_per_step]
    v_refs = rest[pages_per_step:2 * pages_per_step]
    o_ref, m_ref, l_ref, acc_ref = rest[2 * pages_per_step:]
    step = pl.program_id(1)

    qm_f32 = _stack_maps(q_ref[...].astype(F32))
    qm = qm_f32.astype(BF16)

    @pl.when(step == 0)
    def _():
        m_ref[...] = jnp.full(m_ref.shape, NEG_INF, F32)
        l_ref[...] = jnp.zeros(l_ref.shape, F32)
        acc_ref[...] = jnp.zeros(acc_ref.shape, F32)

    def merge(parts):
        m_prev = m_ref[...]
        m_new = functools.reduce(jnp.maximum, [m for m, _, _ in parts], m_prev)
        corr = jnp.exp(m_prev - m_new)
        l_new = corr * l_ref[...]
        acc_new = corr * acc_ref[...]
        for m, l, acc in parts:
            w = jnp.exp(m - m_new)
            l_new = l_new + w * l
            acc_new = acc_new + w * acc
        m_ref[...] = m_new
        l_ref[...] = l_new
        acc_ref[...] = acc_new

    parts = []
    for g0 in range(0, pages_per_step, pages_per_group):
        group = slice(g0, g0 + pages_per_group)
        k_cat = jnp.concatenate([r[...].astype(BF16) for r in k_refs[group]], axis=0)
        v_cat = jnp.concatenate([r[...].astype(BF16) for r in v_refs[group]], axis=0)
        s = lax.dot_general(qm, k_cat, (((1,), (1,)), ((), ())), preferred_element_type=F32)
        row = lax.broadcasted_iota(jnp.int32, s.shape, 0)
        col = lax.broadcasted_iota(jnp.int32, s.shape, 1)
        s = jnp.where(col % N_HEADS == row % N_HEADS, s, NEG_INF)
        m = jnp.max(s, axis=-1, keepdims=True)
        p = jnp.exp(s - m)
        parts.append((m, jnp.sum(p, axis=-1, keepdims=True), _dot(p.astype(BF16), v_cat)))
    merge(parts)

    @pl.when(step == pl.num_programs(1) - 1)
    def _():
        kn = kn_ref[...].astype(BF16).astype(F32)
        vn = vn_ref[...].astype(BF16).astype(F32)
        s_self = jnp.sum(qm_f32 * jnp.concatenate([kn, kn], axis=0), axis=-1, keepdims=True)
        merge([(s_self, jnp.ones_like(s_self), jnp.concatenate([vn, vn], axis=0))])

        lam = _lambda(lq1_ref, lk1_ref, lq2_ref, lk2_ref, lam_init)
        o_all = acc_ref[...] / l_ref[...]
        o = o_all[:N_HEADS] - lam * o_all[N_HEADS:]
        o_ref[...] = _sub_ln(o, g_ref[...], lam_init).astype(o_ref.dtype)


def _decode_attn(q, k_new, v_new, cache_k, cache_v, page_table, lq1, lk1, lq2, lk2, g, *,
                 layer, lam_init, pages_per_step, pages_per_group):
    n_seq, n_pages = page_table.shape
    page_rows = cache_k.shape[2]
    n_steps = n_pages // pages_per_step
    small = lambda a: pl.BlockSpec(a.shape, lambda s, p, pt: (0,) * a.ndim)
    per_seq = pl.BlockSpec((None, N_HEADS, HEAD_W), lambda s, p, pt: (s, 0, 0))

    def page_spec(j):
        return pl.BlockSpec((None, None, page_rows, HEAD_W),
                            lambda s, p, pt: (layer, pt[s, p * pages_per_step + j], 0, 0))

    pages = [page_spec(j) for j in range(pages_per_step)]
    return pl.pallas_call(
        functools.partial(_decode_attn_kernel, lam_init=lam_init, pages_per_step=pages_per_step,
                          pages_per_group=pages_per_group),
        grid_spec=pltpu.PrefetchScalarGridSpec(
            num_scalar_prefetch=1,
            grid=(n_seq, n_steps),
            in_specs=[small(lq1), small(lk1), small(lq2), small(lk2), small(g), per_seq, per_seq, per_seq]
                     + pages + pages,
            out_specs=per_seq,
            scratch_shapes=[pltpu.VMEM((2 * N_HEADS, 1), F32), pltpu.VMEM((2 * N_HEADS, 1), F32),
                            pltpu.VMEM((2 * N_HEADS, HEAD_W), F32)],
        ),
        out_shape=jax.ShapeDtypeStruct((n_seq, N_HEADS, HEAD_W), BF16),
        compiler_params=_compiler_params(("parallel", "arbitrary")),
        name="decode_attn",
    )(page_table, lq1, lk1, lq2, lk2, g, q, k_new, v_new,
      *([cache_k] * pages_per_step), *([cache_v] * pages_per_step))


def _gelu_tanh(x):
    return 0.5 * x * (1.0 + jnp.tanh(math.sqrt(2.0 / math.pi) * (x + 0.044715 * (x * x * x))))


def _lru_coefficients(xc, wa_ref, ba_ref, wx_ref, bx_ref, lam_ref):
    xcb = xc.astype(BF16)
    r = jax.nn.sigmoid(_dot(xcb, wa_ref[...]) + ba_ref[...])
    i = jax.nn.sigmoid(_dot(xcb, wx_ref[...]) + bx_ref[...])
    log_a = -LRU_C * r * jax.nn.softplus(-lam_ref[...])
    a = jnp.exp(log_a)
    b = jnp.sqrt(-jnp.tanh(log_a) * (a * a + 1.0)) * i * xc
    return a, b


def _rglru_seq_kernel(xr_ref, xg_ref, conv0_ref, h0_ref, cw_ref, cb_ref, wa_ref, ba_ref, wx_ref, bx_ref, lam_ref,
                      y_ref, convn_ref, hn_ref, xs_ref, a_ref, b_ref, h_ref, *, tt, nb):
    t = pl.program_id(0)
    d_rec = xr_ref.shape[-1]
    pad = 8
    hist = CONV_W - 1

    @pl.when(t == 0)
    def _():
        xs_ref[:, pad - hist:pad, :] = conv0_ref[...]
        h_ref[...] = h0_ref[...]

    @pl.when(t > 0)
    def _():
        xs_ref[:, pad - hist:pad, :] = xs_ref[:, pad + tt - hist:pad + tt, :]

    xs_ref[:, pad:, :] = xr_ref[...]
    xc = cb_ref[...][None]
    for j in range(CONV_W):
        xc = xc + xs_ref[:, pad - hist + j:pad - hist + j + tt, :] * cw_ref[j:j + 1, :][None]
    xc = xc.reshape(nb * tt, d_rec)

    a, b = _lru_coefficients(xc, wa_ref, ba_ref, wx_ref, bx_ref, lam_ref)
    n_chunks = d_rec // LANES
    for c in range(n_chunks):
        a_ref[c] = a[:, c * LANES:(c + 1) * LANES]
        b_ref[c] = b[:, c * LANES:(c + 1) * LANES]

    def step(s, hs):
        rows = pl.ds(s, nb, stride=tt)
        new = []
        for c in range(n_chunks):
            h = a_ref[c, rows, :] * hs[c] + b_ref[c, rows, :]
            b_ref[c, rows, :] = h
            new.append(h)
        return tuple(new)

    h0 = h_ref[...]
    hs = lax.fori_loop(0, tt, step, tuple(h0[:, c * LANES:(c + 1) * LANES] for c in range(n_chunks)), unroll=8)
    h = jnp.concatenate(hs, axis=-1)
    h_ref[...] = h

    y = jnp.concatenate([b_ref[c] for c in range(n_chunks)], axis=-1) * _gelu_tanh(
        xg_ref[...].reshape(nb * tt, d_rec))
    y_ref[...] = y.reshape(nb, tt, d_rec).astype(y_ref.dtype)
    convn_ref[...] = xs_ref[:, pad + tt - hist:pad + tt, :]
    hn_ref[...] = h


def _rglru_seq(xr, xg, conv0, h0, cw, cb, wa, ba, wx, bx, lam, *, tt):
    nb, seq, d_rec = xr.shape
    tile = pl.BlockSpec((nb, tt, d_rec), lambda t: (0, t, 0))
    whole = lambda a: pl.BlockSpec(a.shape, lambda t: (0,) * a.ndim)
    return pl.pallas_call(
        functools.partial(_rglru_seq_kernel, tt=tt, nb=nb),
        grid=(seq // tt,),
        in_specs=[tile, tile] + [whole(a) for a in (conv0, h0, cw, cb, wa, ba, wx, bx, lam)],
        out_specs=[tile, whole(conv0), whole(h0)],
        out_shape=[jax.ShapeDtypeStruct((nb, seq, d_rec), BF16),
                   jax.ShapeDtypeStruct(conv0.shape, F32), jax.ShapeDtypeStruct(h0.shape, F32)],
        scratch_shapes=[pltpu.VMEM((nb, 8 + tt, d_rec), F32), pltpu.VMEM((d_rec // LANES, nb * tt, LANES), F32),
                        pltpu.VMEM((d_rec // LANES, nb * tt, LANES), F32), pltpu.VMEM((nb, d_rec), F32)],
        compiler_params=_compiler_params(("arbitrary",)),
        name="rglru_seq",
    )(xr, xg, conv0, h0, cw, cb, wa, ba, wx, bx, lam)


def _rglru_step_kernel(xr_ref, xg_ref, conv_ref, h0_ref, cw_ref, cb_ref, wa_ref, ba_ref, wx_ref, bx_ref, lam_ref,
                       y_ref, convn_ref, hn_ref):
    xr = xr_ref[...]
    xc = cb_ref[...] + xr * cw_ref[CONV_W - 1:CONV_W, :]
    for j in range(CONV_W - 1):
        xc = xc + conv_ref[j] * cw_ref[j:j + 1, :]
    a, b = _lru_coefficients(xc, wa_ref, ba_ref, wx_ref, bx_ref, lam_ref)
    h = a * h0_ref[...] + b
    y_ref[...] = (h * _gelu_tanh(xg_ref[...])).astype(y_ref.dtype)
    for j in range(CONV_W - 2):
        convn_ref[j] = conv_ref[j + 1]
    convn_ref[CONV_W - 2] = xr
    hn_ref[...] = h


def _rglru_step(xr, xg, conv_t, h0, cw, cb, wa, ba, wx, bx, lam):
    args = (xr, xg, conv_t, h0, cw, cb, wa, ba, wx, bx, lam)
    whole = lambda a: pl.BlockSpec(a.shape, lambda: (0,) * a.ndim)
    return pl.pallas_call(
        _rglru_step_kernel,
        in_specs=[whole(a) for a in args],
        out_specs=[whole(xr), whole(conv_t), whole(h0)],
        out_shape=[jax.ShapeDtypeStruct(xr.shape, BF16), jax.ShapeDtypeStruct(conv_t.shape, F32),
                   jax.ShapeDtypeStruct(h0.shape, F32)],
        name="rglru_step",
    )(*args)


def _block_diag(w):
    n_blk, blk, _ = w.shape
    eye = jnp.eye(n_blk, dtype=w.dtype)
    return jnp.einsum("nde,nm->ndme", w, eye).reshape(n_blk * blk, n_blk * blk)


def _row_tile(n, preferred):
    return preferred if n % preferred == 0 else n


def kernel(x_prompt, x_sample, cache_k, cache_v, state_conv, state_h, page_table, ln1_g, ln1_b, ln2_g, ln2_b, ln3_g, ln3_b, ffn1_w_gate, ffn1_w_up, ffn1_w_down, ffn2_w_gate, ffn2_w_up, ffn2_w_down, w_in, w_out, attn_lam_q1, attn_lam_k1, attn_lam_q2, attn_lam_k2, attn_subln_g, conv_w, conv_b, gate_a_w, gate_a_b, gate_x_w, gate_x_b, lru_lambda):
    depth = w_in.shape[0]
    batch, seq, d_model = x_prompt.shape
    n_seq = x_sample.shape[0]
    assert x_sample.shape[1] == 1, "the sample group decodes one token per sequence"
    d_attn = N_HEADS * HEAD_W
    d_rec = d_model - d_attn
    alpha = (2 * depth) ** 0.25
    n_pool, page = cache_k.shape[1:3]
    n_tok = batch * seq

    cache_k2 = cache_k.reshape(depth, n_pool, page * N_HEADS, HEAD_W)
    cache_v2 = cache_v.reshape(depth, n_pool, page * N_HEADS, HEAD_W)
    bf = lambda w: w.astype(BF16)
    vec = lambda p, l: p[l][None, :]

    xp = x_prompt.reshape(n_tok, d_model)
    xs = x_sample.reshape(n_seq, d_model)
    conv_p = jnp.zeros((batch, CONV_W - 1, d_rec), F32)
    h_p = jnp.zeros((batch, d_rec), F32)
    kp_all = jnp.zeros((depth, n_tok, N_HEADS, HEAD_W), F32)
    vp_all = jnp.zeros((depth, n_tok, N_HEADS, HEAD_W), F32)
    ks_all = jnp.zeros((depth, n_seq, N_HEADS, HEAD_W), F32)
    vs_all = jnp.zeros((depth, n_seq, N_HEADS, HEAD_W), F32)

    tm = _row_tile(n_tok, 512)
    tq = _row_tile(seq, 256)
    tt = _row_tile(seq, 128)
    pages_per_step = next(p for p in (16, 8, 4, 2, 1) if page_table.shape[1] % p == 0)
    pages_per_group = min(4, pages_per_step)

    ffn1_w = (bf(ffn1_w_gate), bf(ffn1_w_up), bf(ffn1_w_down))
    ffn2_w = (bf(ffn2_w_gate), bf(ffn2_w_up), bf(ffn2_w_down))
    w_in_bf, w_out_bf = bf(w_in), bf(w_out)

    outs = {name: [] for name in ("cp", "hp", "cs", "hs")}
    for l in range(depth):
        lam_init = 0.8 - 0.6 * math.exp(-0.3 * l)
        ffn1 = (*ffn1_w, vec(ln1_g, l), vec(ln1_b, l))
        ffn2 = (*ffn2_w, vec(ln3_g, l), vec(ln3_b, l))
        lam_vecs = (vec(attn_lam_q1, l), vec(attn_lam_k1, l), vec(attn_lam_q2, l), vec(attn_lam_k2, l))
        sub_g = vec(attn_subln_g, l)
        rec = (conv_w[l], vec(conv_b, l), bf(_block_diag(gate_a_w[l])), vec(gate_a_b, l),
               bf(_block_diag(gate_x_w[l])), vec(gate_x_b, l), vec(lru_lambda, l))

        xp, q, kp_all, vp_all, kb, vb, xr, xg = _ffn_proj(xp, *ffn1, w_in_bf, kp_all, vp_all, layer=l, alpha=alpha,
                                                          d_attn=d_attn, d_rec=d_rec, tm=tm)
        attn = _prompt_attn(q, kb, vb, *lam_vecs, sub_g, lam_init=lam_init, batch=batch, seq=seq, tq=tq,
                            heads_per_step=N_HEADS)
        y, conv_n, h_n = _rglru_seq(xr.reshape(batch, seq, d_rec), xg.reshape(batch, seq, d_rec),
                                    conv_p, h_p, *rec, tt=tt)
        xp = _mix_ffn(xp, attn, y.reshape(n_tok, d_rec), w_out_bf, vec(ln2_g, l), vec(ln2_b, l), *ffn2,
                      layer=l, alpha=alpha, tm=tm)
        outs["cp"].append(conv_n)
        outs["hp"].append(h_n)

        xs, q, ks_all, vs_all, _, _, xr, xg = _ffn_proj(xs, *ffn1, w_in_bf, ks_all, vs_all, layer=l, alpha=alpha,
                                                        d_attn=d_attn, d_rec=d_rec, tm=n_seq)
        attn = _decode_attn(q.reshape(n_seq, N_HEADS, HEAD_W), ks_all[l], vs_all[l], cache_k2, cache_v2, page_table,
                            *lam_vecs, sub_g, layer=l, lam_init=lam_init, pages_per_step=pages_per_step,
                            pages_per_group=pages_per_group)
        y, conv_n, h_n = _rglru_step(xr, xg, jnp.swapaxes(state_conv[l], 0, 1), state_h[l], *rec)
        xs = _mix_ffn(xs, attn.reshape(n_seq, d_attn), y, w_out_bf, vec(ln2_g, l), vec(ln2_b, l), *ffn2,
                      layer=l, alpha=alpha, tm=n_seq)
        outs["cs"].append(jnp.swapaxes(conv_n, 0, 1))
        outs["hs"].append(h_n)

    stack = lambda name: jnp.stack(outs[name])
    return (xp.reshape(batch, seq, d_model), xs.reshape(n_seq, 1, d_model),
            kp_all.reshape(depth, batch, seq, N_HEADS, HEAD_W), vp_all.reshape(depth, batch, seq, N_HEADS, HEAD_W),
            stack("cp"), stack("hp"),
            ks_all.reshape(depth, n_seq, 1, N_HEADS, HEAD_W), vs_all.reshape(depth, n_seq, 1, N_HEADS, HEAD_W),
            stack("cs"), stack("hs"))
```

```python
import functools
import math

import jax
import jax.numpy as jnp
from jax import lax
from jax.experimental import pallas as pl
from jax.experimental.pallas import tpu as pltpu

F32 = jnp.float32
BF16 = jnp.bfloat16

LANES = 128
N_HEADS = 4
HEAD_DIM = 64
HEAD_W = 2 * HEAD_DIM
CONV_W = 4
LRU_C = 8.0
LN_EPS = 1e-5
NEG_INF = -1e30
QK_SCALE = HEAD_DIM ** -0.5

VMEM_LIMIT_BYTES = 56 * 1024 * 1024


def _compiler_params(semantics):
    return pltpu.CompilerParams(dimension_semantics=semantics, vmem_limit_bytes=VMEM_LIMIT_BYTES)


def _resident(shape):
    return pl.BlockSpec(shape, lambda *_: (0,) * len(shape), pipeline_mode=pl.Buffered(1))


def _layer_weight(shape, layer):
    return pl.BlockSpec((None,) + shape, lambda *_: (layer,) + (0,) * len(shape), pipeline_mode=pl.Buffered(1))


def _layer_norm(y, g, b):
    mu = jnp.mean(y, axis=-1, keepdims=True)
    d = y - mu
    var = jnp.mean(d * d, axis=-1, keepdims=True)
    return d * lax.rsqrt(var + LN_EPS) * g + b


def _dot(a, b):
    return jnp.dot(a, b, preferred_element_type=F32)


def _swiglu_ln(x, wg_ref, wu_ref, wd_ref, g_ref, b_ref, alpha):
    xb = x.astype(BF16)
    gate = _dot(xb, wg_ref[...])
    up = _dot(xb, wu_ref[...])
    h = (gate * jax.nn.sigmoid(gate) * up).astype(BF16)
    return _layer_norm(alpha * x + 0.5 * _dot(h, wd_ref[...]), g_ref[...], b_ref[...])


def _ffn_proj_kernel(x_ref, wg_ref, wu_ref, wd_ref, g_ref, b_ref, w_in_ref, k_all_in, v_all_in,
                     x1_ref, q_ref, k_all_ref, v_all_ref, kb_ref, vb_ref, xr_ref, xg_ref, *, alpha, d_attn, d_rec):
    del k_all_in, v_all_in
    x1 = _swiglu_ln(x_ref[...], wg_ref, wu_ref, wd_ref, g_ref, b_ref, alpha)
    x1_ref[...] = x1
    proj = _dot(x1.astype(BF16), w_in_ref[...])
    q = proj[:, :d_attn]
    k = proj[:, d_attn:2 * d_attn]
    v = proj[:, 2 * d_attn:3 * d_attn]
    q_ref[...] = (q * QK_SCALE).astype(BF16)
    for h in range(N_HEADS):
        k_all_ref[:, h, :] = k[:, h * HEAD_W:(h + 1) * HEAD_W]
        v_all_ref[:, h, :] = v[:, h * HEAD_W:(h + 1) * HEAD_W]
    kb_ref[...] = k.astype(BF16)
    vb_ref[...] = v.astype(BF16)
    xr_ref[...] = proj[:, 3 * d_attn:3 * d_attn + d_rec]
    xg_ref[...] = proj[:, 3 * d_attn + d_rec:]


def _ffn_proj(x, wg, wu, wd, g, b, w_in, k_all, v_all, *, layer, alpha, d_attn, d_rec, tm):
    n, d = x.shape
    d_ff = wg.shape[-1]
    d_in = w_in.shape[-1]
    weight = functools.partial(_layer_weight, layer=layer)
    row = lambda w: pl.BlockSpec((tm, w), lambda i: (i, 0))
    sds = lambda w, dt: jax.ShapeDtypeStruct((n, w), dt)
    layer_rows = pl.BlockSpec((None, tm, N_HEADS, HEAD_W), lambda i: (layer, i, 0, 0))
    in_hbm = pl.BlockSpec(memory_space=pl.ANY)
    return pl.pallas_call(
        functools.partial(_ffn_proj_kernel, alpha=alpha, d_attn=d_attn, d_rec=d_rec),
        grid=(n // tm,),
        in_specs=[row(d), weight((d, d_ff)), weight((d, d_ff)), weight((d_ff, d)),
                  _resident((1, d)), _resident((1, d)), weight((d, d_in)), in_hbm, in_hbm],
        out_specs=[row(d), row(d_attn), layer_rows, layer_rows, row(d_attn), row(d_attn), row(d_rec), row(d_rec)],
        out_shape=[sds(d, F32), sds(d_attn, BF16), jax.ShapeDtypeStruct(k_all.shape, F32),
                   jax.ShapeDtypeStruct(v_all.shape, F32), sds(d_attn, BF16), sds(d_attn, BF16),
                   sds(d_rec, F32), sds(d_rec, F32)],
        input_output_aliases={7: 2, 8: 3},
        compiler_params=_compiler_params(("parallel",)),
        name="ffn_proj",
    )(x, wg, wu, wd, g, b, w_in, k_all, v_all)


def _mix_ffn_kernel(x_ref, attn_ref, rec_ref, w_out_ref, g2_ref, b2_ref, wg_ref, wu_ref, wd_ref, g3_ref, b3_ref,
                    o_ref, *, alpha):
    mixed = jnp.concatenate([attn_ref[...], rec_ref[...]], axis=-1)
    x2 = _layer_norm(alpha * x_ref[...] + _dot(mixed, w_out_ref[...]), g2_ref[...], b2_ref[...])
    o_ref[...] = _swiglu_ln(x2, wg_ref, wu_ref, wd_ref, g3_ref, b3_ref, alpha)


def _mix_ffn(x, attn, rec, w_out, g2, b2, wg, wu, wd, g3, b3, *, layer, alpha, tm):
    n, d = x.shape
    d_ff = wg.shape[-1]
    weight = functools.partial(_layer_weight, layer=layer)
    row = lambda w: pl.BlockSpec((tm, w), lambda i: (i, 0))
    vec_spec = _resident((1, d))
    return pl.pallas_call(
        functools.partial(_mix_ffn_kernel, alpha=alpha),
        grid=(n // tm,),
        in_specs=[row(d), row(attn.shape[1]), row(rec.shape[1]), weight((d, d)), vec_spec, vec_spec,
                  weight((d, d_ff)), weight((d, d_ff)), weight((d_ff, d)), vec_spec, vec_spec],
        out_specs=row(d),
        out_shape=jax.ShapeDtypeStruct((n, d), F32),
        compiler_params=_compiler_params(("parallel",)),
        name="mix_ffn",
    )(x, attn, rec, w_out, g2, b2, wg, wu, wd, g3, b3)


def _lambda(lq1_ref, lk1_ref, lq2_ref, lk2_ref, lam_init):
    s1 = jnp.sum(lq1_ref[...] * lk1_ref[...], axis=-1, keepdims=True)
    s2 = jnp.sum(lq2_ref[...] * lk2_ref[...], axis=-1, keepdims=True)
    return jnp.exp(s1) - jnp.exp(s2) + lam_init


def _sub_ln(o, g, lam_init):
    ms = jnp.mean(o * o, axis=-1, keepdims=True)
    return o * lax.rsqrt(ms + LN_EPS) * g * (1.0 - lam_init)


def _stack_maps(q):
    lane = lax.broadcasted_iota(jnp.int32, q.shape, 1)
    zero = jnp.zeros_like(q)
    return jnp.concatenate([jnp.where(lane < HEAD_DIM, q, zero), jnp.where(lane >= HEAD_DIM, q, zero)], axis=0)


def _prompt_attn_kernel(lq1_ref, lk1_ref, lq2_ref, lk2_ref, g_ref, q_ref, k_ref, v_ref, o_ref, vx_ref,
                        *, lam_init, tq, nq, n_heads):
    qi = pl.program_id(2)

    @pl.when(qi == 0)
    def _():
        for h in range(n_heads):
            vx_ref[h, :, :HEAD_W] = v_ref[:, h * HEAD_W:(h + 1) * HEAD_W]
            vx_ref[h, :, HEAD_W:] = jnp.ones((vx_ref.shape[1], HEAD_W), BF16)

    lam = _lambda(lq1_ref, lk1_ref, lq2_ref, lk2_ref, lam_init)
    row = lax.broadcasted_iota(jnp.int32, (2 * tq, tq), 0)
    col = lax.broadcasted_iota(jnp.int32, (2 * tq, tq), 1)
    causal = col <= jnp.where(row >= tq, row - tq, row)

    def attend_head(h, n_keys):
        lanes = slice(h * HEAD_W, (h + 1) * HEAD_W)
        qs = _stack_maps(q_ref[:, lanes])
        s = lax.dot_general(qs, k_ref[:n_keys, lanes], (((1,), (1,)), ((), ())), preferred_element_type=F32)
        s_diag = jnp.where(causal, s[:, n_keys - tq:], NEG_INF)
        chunks = [s[:, c * LANES:(c + 1) * LANES] for c in range((n_keys - tq) // LANES)]
        chunks += [s_diag[:, c * LANES:(c + 1) * LANES] for c in range(tq // LANES)]
        m = jnp.max(functools.reduce(jnp.maximum, chunks), axis=-1, keepdims=True)
        m_b = jnp.broadcast_to(m, (2 * tq, LANES))
        p = jnp.concatenate([jnp.exp(c - m_b) for c in chunks], axis=-1).astype(BF16)
        acc = _dot(p, vx_ref[h, :n_keys, :])
        o_all = acc[:, :HEAD_W] / acc[:, HEAD_W:]
        o = o_all[:tq] - lam * o_all[tq:]
        return _sub_ln(o, g_ref[...], lam_init).astype(o_ref.dtype)

    def attend(n_keys):
        o_ref[...] = jnp.concatenate([attend_head(h, n_keys) for h in range(n_heads)], axis=-1)

    for i in range(nq):
        pl.when(qi == i)(functools.partial(attend, (i + 1) * tq))


def _prompt_attn(q, kb, vb, lq1, lk1, lq2, lk2, g, *, lam_init, batch, seq, tq, heads_per_step):
    n, d_attn = q.shape
    nq = seq // tq
    width = heads_per_step * HEAD_W
    small = lambda a: _resident(a.shape)
    return pl.pallas_call(
        functools.partial(_prompt_attn_kernel, lam_init=lam_init, tq=tq, nq=nq, n_heads=heads_per_step),
        grid=(batch, N_HEADS // heads_per_step, nq),
        in_specs=[
            small(lq1), small(lk1), small(lq2), small(lk2), small(g),
            pl.BlockSpec((tq, width), lambda b, h, i: (b * nq + i, h)),
            pl.BlockSpec((seq, width), lambda b, h, i: (b, h)),
            pl.BlockSpec((seq, width), lambda b, h, i: (b, h)),
        ],
        out_specs=pl.BlockSpec((tq, width), lambda b, h, i: (b * nq + i, h)),
        out_shape=jax.ShapeDtypeStruct((n, d_attn), BF16),
        scratch_shapes=[pltpu.VMEM((heads_per_step, seq, 2 * HEAD_W), BF16)],
        compiler_params=_compiler_params(("parallel", "parallel", "arbitrary")),
        name="prompt_attn",
    )(lq1, lk1, lq2, lk2, g, q, kb, vb)


def _decode_attn_kernel(pt_ref, lq1_ref, lk1_ref, lq2_ref, lk2_ref, g_ref, q_ref, kn_ref, vn_ref, *rest,
                        lam_init, pages_per_step, pages_per_group):
    del pt_ref
    k_refs = rest[:pages_per_step]
    v_refs = rest[pages_per_step:2 * pages_per_step]
    o_ref, m_ref, l_ref, acc_ref = rest[2 * pages_per_step:]
    step = pl.program_id(1)

    qm_f32 = _stack_maps(q_ref[...].astype(F32))
    qm = qm_f32.astype(BF16)

    @pl.when(step == 0)
    def _():
        m_ref[...] = jnp.full(m_ref.shape, NEG_INF, F32)
        l_ref[...] = jnp.zeros(l_ref.shape, F32)
        acc_ref[...] = jnp.zeros(acc_ref.shape, F32)

    def merge(parts):
        m_prev = m_ref[...]
        m_new = functools.reduce(jnp.maximum, [m for m, _, _ in parts], m_prev)
        corr = jnp.exp(m_prev - m_new)
        l_new = corr * l_ref[...]
        acc_new = corr * acc_ref[...]
        for m, l, acc in parts:
            w = jnp.exp(m - m_new)
            l_new = l_new + w * l
            acc_new = acc_new + w * acc
        m_ref[...] = m_new
        l_ref[...] = l_new
        acc_ref[...] = acc_new

    parts = []
    for g0 in range(0, pages_per_step, pages_per_group):
        group = slice(g0, g0 + pages_per_group)
        k_cat = jnp.concatenate([r[...].astype(BF16) for r in k_refs[group]], axis=0)
        v_cat = jnp.concatenate([r[...].astype(BF16) for r in v_refs[group]], axis=0)
        s = lax.dot_general(qm, k_cat, (((1,), (1,)), ((), ())), preferred_element_type=F32)
        row = lax.broadcasted_iota(jnp.int32, s.shape, 0)
        col = lax.broadcasted_iota(jnp.int32, s.shape, 1)
        s = jnp.where(col % N_HEADS == row % N_HEADS, s, NEG_INF)
        m = jnp.max(s, axis=-1, keepdims=True)
        p = jnp.exp(s - m)
        parts.append((m, jnp.sum(p, axis=-1, keepdims=True), _dot(p.astype(BF16), v_cat)))
    merge(parts)

    @pl.when(step == pl.num_programs(1) - 1)
    def _():
        kn = kn_ref[...].astype(BF16).astype(F32)
        vn = vn_ref[...].astype(BF16).astype(F32)
        s_self = jnp.sum(qm_f32 * jnp.concatenate([kn, kn], axis=0), axis=-1, keepdims=True)
        merge([(s_self, jnp.ones_like(s_self), jnp.concatenate([vn, vn], axis=0))])

        lam = _lambda(lq1_ref, lk1_ref, lq2_ref, lk2_ref, lam_init)
        o_all = acc_ref[...] / l_ref[...]
        o = o_all[:N_HEADS] - lam * o_all[N_HEADS:]
        o_ref[...] = _sub_ln(o, g_ref[...], lam_init).astype(o_ref.dtype)


def _decode_attn(q, k_new, v_new, cache_k, cache_v, page_table, lq1, lk1, lq2, lk2, g, *,
                 layer, lam_init, pages_per_step, pages_per_group):
    n_seq, n_pages = page_table.shape
    page_rows = cache_k.shape[2]
    n_steps = n_pages // pages_per_step
    small = lambda a: pl.BlockSpec(a.shape, lambda s, p, pt: (0,) * a.ndim)
    per_seq = pl.BlockSpec((None, N_HEADS, HEAD_W), lambda s, p, pt: (s, 0, 0))

    def page_spec(j):
        return pl.BlockSpec((None, None, page_rows, HEAD_W),
                            lambda s, p, pt: (layer, pt[s, p * pages_per_step + j], 0, 0))

    pages = [page_spec(j) for j in range(pages_per_step)]
    return pl.pallas_call(
        functools.partial(_decode_attn_kernel, lam_init=lam_init, pages_per_step=pages_per_step,
                          pages_per_group=pages_per_group),
        grid_spec=pltpu.PrefetchScalarGridSpec(
            num_scalar_prefetch=1,
            grid=(n_seq, n_steps),
            in_specs=[small(lq1), small(lk1), small(lq2), small(lk2), small(g), per_seq, per_seq, per_seq]
                     + pages + pages,
            out_specs=per_seq,
            scratch_shapes=[pltpu.VMEM((2 * N_HEADS, 1), F32), pltpu.VMEM((2 * N_HEADS, 1), F32),
                            pltpu.VMEM((2 * N_HEADS, HEAD_W), F32)],
        ),
        out_shape=jax.ShapeDtypeStruct((n_seq, N_HEADS, HEAD_W), BF16),
        compiler_params=_compiler_params(("parallel", "arbitrary")),
        name="decode_attn",
    )(page_table, lq1, lk1, lq2, lk2, g, q, k_new, v_new,
      *([cache_k] * pages_per_step), *([cache_v] * pages_per_step))


def _gelu_tanh(x):
    return 0.5 * x * (1.0 + jnp.tanh(math.sqrt(2.0 / math.pi) * (x + 0.044715 * (x * x * x))))


def _lru_coefficients(xc, wa_ref, ba_ref, wx_ref, bx_ref, lam_ref):
    xcb = xc.astype(BF16)
    r = jax.nn.sigmoid(_dot(xcb, wa_ref[...]) + ba_ref[...])
    i = jax.nn.sigmoid(_dot(xcb, wx_ref[...]) + bx_ref[...])
    log_a = -LRU_C * r * jax.nn.softplus(-lam_ref[...])
    a = jnp.exp(log_a)
    b = jnp.sqrt(-jnp.tanh(log_a) * (a * a + 1.0)) * i * xc
    return a, b


def _rglru_seq_kernel(xr_ref, xg_ref, conv0_ref, h0_ref, cw_ref, cb_ref, wa_ref, ba_ref, wx_ref, bx_ref, lam_ref,
                      y_ref, convn_ref, hn_ref, xs_ref, a_ref, b_ref, h_ref, *, tt, nb):
    t = pl.program_id(0)
    d_rec = xr_ref.shape[-1]
    hist = CONV_W - 1

    @pl.when(t == 0)
    def _():
        xs_ref[:hist] = conv0_ref[...]
        h_ref[...] = h0_ref[...]

    @pl.when(t > 0)
    def _():
        xs_ref[:hist] = xs_ref[tt:tt + hist]

    xs_ref[hist:] = jnp.swapaxes(xr_ref[...], 0, 1)
    xc = cb_ref[...][None]
    for j in range(CONV_W):
        xc = xc + xs_ref[j:j + tt] * cw_ref[j:j + 1, :][None]

    a, b = _lru_coefficients(xc.reshape(tt * nb, d_rec), wa_ref, ba_ref, wx_ref, bx_ref, lam_ref)
    a_ref[...] = a.reshape(tt, nb, d_rec)
    b_ref[...] = b.reshape(tt, nb, d_rec)

    def step(s, h):
        h = a_ref[s] * h + b_ref[s]
        b_ref[s] = h
        return h

    h = lax.fori_loop(0, tt, step, h_ref[...], unroll=8)
    h_ref[...] = h

    y = b_ref[...] * _gelu_tanh(jnp.swapaxes(xg_ref[...], 0, 1))
    y_ref[...] = jnp.swapaxes(y, 0, 1).astype(y_ref.dtype)
    convn_ref[...] = xs_ref[tt:tt + hist]
    hn_ref[...] = h


def _rglru_seq(xr, xg, conv0, h0, cw, cb, wa, ba, wx, bx, lam, *, tt):
    nb, seq, d_rec = xr.shape
    tile = pl.BlockSpec((nb, tt, d_rec), lambda t: (0, t, 0))
    whole = lambda a: pl.BlockSpec(a.shape, lambda t: (0,) * a.ndim)
    return pl.pallas_call(
        functools.partial(_rglru_seq_kernel, tt=tt, nb=nb),
        grid=(seq // tt,),
        in_specs=[tile, tile] + [whole(a) for a in (conv0, h0, cw, cb, wa, ba, wx, bx, lam)],
        out_specs=[tile, whole(conv0), whole(h0)],
        out_shape=[jax.ShapeDtypeStruct((nb, seq, d_rec), BF16),
                   jax.ShapeDtypeStruct(conv0.shape, F32), jax.ShapeDtypeStruct(h0.shape, F32)],
        scratch_shapes=[pltpu.VMEM((CONV_W - 1 + tt, nb, d_rec), F32), pltpu.VMEM((tt, nb, d_rec), F32),
                        pltpu.VMEM((tt, nb, d_rec), F32), pltpu.VMEM((nb, d_rec), F32)],
        compiler_params=_compiler_params(("arbitrary",)),
        name="rglru_seq",
    )(xr, xg, conv0, h0, cw, cb, wa, ba, wx, bx, lam)


def _rglru_step_kernel(xr_ref, xg_ref, conv_ref, h0_ref, cw_ref, cb_ref, wa_ref, ba_ref, wx_ref, bx_ref, lam_ref,
                       y_ref, convn_ref, hn_ref):
    xr = xr_ref[...]
    xc = cb_ref[...] + xr * cw_ref[CONV_W - 1:CONV_W, :]
    for j in range(CONV_W - 1):
        xc = xc + conv_ref[j] * cw_ref[j:j + 1, :]
    a, b = _lru_coefficients(xc, wa_ref, ba_ref, wx_ref, bx_ref, lam_ref)
    h = a * h0_ref[...] + b
    y_ref[...] = (h * _gelu_tanh(xg_ref[...])).astype(y_ref.dtype)
    for j in range(CONV_W - 2):
        convn_ref[j] = conv_ref[j + 1]
    convn_ref[CONV_W - 2] = xr
    hn_ref[...] = h


def _rglru_step(xr, xg, conv_t, h0, cw, cb, wa, ba, wx, bx, lam):
    args = (xr, xg, conv_t, h0, cw, cb, wa, ba, wx, bx, lam)
    whole = lambda a: pl.BlockSpec(a.shape, lambda: (0,) * a.ndim)
    return pl.pallas_call(
        _rglru_step_kernel,
        in_specs=[whole(a) for a in args],
        out_specs=[whole(xr), whole(conv_t), whole(h0)],
        out_shape=[jax.ShapeDtypeStruct(xr.shape, BF16), jax.ShapeDtypeStruct(conv_t.shape, F32),
                   jax.ShapeDtypeStruct(h0.shape, F32)],
        name="rglru_step",
    )(*args)


def _block_diag(w):
    n_blk, blk, _ = w.shape
    eye = jnp.eye(n_blk, dtype=w.dtype)
    return jnp.einsum("nde,nm->ndme", w, eye).reshape(n_blk * blk, n_blk * blk)


def _row_tile(n, preferred):
    return preferred if n % preferred == 0 else n


def kernel(x_prompt, x_sample, cache_k, cache_v, state_conv, state_h, page_table, ln1_g, ln1_b, ln2_g, ln2_b, ln3_g, ln3_b, ffn1_w_gate, ffn1_w_up, ffn1_w_down, ffn2_w_gate, ffn2_w_up, ffn2_w_down, w_in, w_out, attn_lam_q1, attn_lam_k1, attn_lam_q2, attn_lam_k2, attn_subln_g, conv_w, conv_b, gate_a_w, gate_a_b, gate_x_w, gate_x_b, lru_lambda):
    depth = w_in.shape[0]
    batch, seq, d_model = x_prompt.shape
    n_seq = x_sample.shape[0]
    assert x_sample.shape[1] == 1, "the sample group decodes one token per sequence"
    d_attn = N_HEADS * HEAD_W
    d_rec = d_model - d_attn
    alpha = (2 * depth) ** 0.25
    n_pool, page = cache_k.shape[1:3]
    n_tok = batch * seq

    cache_k2 = cache_k.reshape(depth, n_pool, page * N_HEADS, HEAD_W)
    cache_v2 = cache_v.reshape(depth, n_pool, page * N_HEADS, HEAD_W)
    bf = lambda w: w.astype(BF16)
    vec = lambda p, l: p[l][None, :]

    xp = x_prompt.reshape(n_tok, d_model)
    xs = x_sample.reshape(n_seq, d_model)
    conv_p = jnp.zeros((CONV_W - 1, batch, d_rec), F32)
    h_p = jnp.zeros((batch, d_rec), F32)
    kp_all = jnp.zeros((depth, n_tok, N_HEADS, HEAD_W), F32)
    vp_all = jnp.zeros((depth, n_tok, N_HEADS, HEAD_W), F32)
    ks_all = jnp.zeros((depth, n_seq, N_HEADS, HEAD_W), F32)
    vs_all = jnp.zeros((depth, n_seq, N_HEADS, HEAD_W), F32)

    tm = _row_tile(n_tok, 512)
    tq = _row_tile(seq, 256)
    tt = _row_tile(seq, 128)
    pages_per_step = next(p for p in (16, 8, 4, 2, 1) if page_table.shape[1] % p == 0)
    pages_per_group = min(4, pages_per_step)

    ffn1_w = (bf(ffn1_w_gate), bf(ffn1_w_up), bf(ffn1_w_down))
    ffn2_w = (bf(ffn2_w_gate), bf(ffn2_w_up), bf(ffn2_w_down))
    w_in_bf, w_out_bf = bf(w_in), bf(w_out)

    outs = {name: [] for name in ("cp", "hp", "cs", "hs")}
    for l in range(depth):
        lam_init = 0.8 - 0.6 * math.exp(-0.3 * l)
        ffn1 = (*ffn1_w, vec(ln1_g, l), vec(ln1_b, l))
        ffn2 = (*ffn2_w, vec(ln3_g, l), vec(ln3_b, l))
        lam_vecs = (vec(attn_lam_q1, l), vec(attn_lam_k1, l), vec(attn_lam_q2, l), vec(attn_lam_k2, l))
        sub_g = vec(attn_subln_g, l)
        rec = (conv_w[l], vec(conv_b, l), bf(_block_diag(gate_a_w[l])), vec(gate_a_b, l),
               bf(_block_diag(gate_x_w[l])), vec(gate_x_b, l), vec(lru_lambda, l))

        xp, q, kp_all, vp_all, kb, vb, xr, xg = _ffn_proj(xp, *ffn1, w_in_bf, kp_all, vp_all, layer=l, alpha=alpha,
                                                          d_attn=d_attn, d_rec=d_rec, tm=tm)
        attn = _prompt_attn(q, kb, vb, *lam_vecs, sub_g, lam_init=lam_init, batch=batch, seq=seq, tq=tq,
                            heads_per_step=N_HEADS)
        y, conv_n, h_n = _rglru_seq(xr.reshape(batch, seq, d_rec), xg.reshape(batch, seq, d_rec),
                                    conv_p, h_p, *rec, tt=tt)
        xp = _mix_ffn(xp, attn, y.reshape(n_tok, d_rec), w_out_bf, vec(ln2_g, l), vec(ln2_b, l), *ffn2,
                      layer=l, alpha=alpha, tm=tm)
        outs["cp"].append(jnp.swapaxes(conv_n, 0, 1))
        outs["hp"].append(h_n)

        xs, q, ks_all, vs_all, _, _, xr, xg = _ffn_proj(xs, *ffn1, w_in_bf, ks_all, vs_all, layer=l, alpha=alpha,
                                                        d_attn=d_attn, d_rec=d_rec, tm=n_seq)
        attn = _decode_attn(q.reshape(n_seq, N_HEADS, HEAD_W), ks_all[l], vs_all[l], cache_k2, cache_v2, page_table,
                            *lam_vecs, sub_g, layer=l, lam_init=lam_init, pages_per_step=pages_per_step,
                            pages_per_group=pages_per_group)
        y, conv_n, h_n = _rglru_step(xr, xg, jnp.swapaxes(state_conv[l], 0, 1), state_h[l], *rec)
        xs = _mix_ffn(xs, attn.reshape(n_seq, d_attn), y, w_out_bf, vec(ln2_g, l), vec(ln2_b, l), *ffn2,
                      layer=l, alpha=alpha, tm=n_seq)
        outs["cs"].append(jnp.swapaxes(conv_n, 0, 1))
        outs["hs"].append(h_n)

    stack = lambda name: jnp.stack(outs[name])
    return (xp.reshape(batch, seq, d_model), xs.reshape(n_seq, 1, d_model),
            kp_all.reshape(depth, batch, seq, N_HEADS, HEAD_W), vp_all.reshape(depth, batch, seq, N_HEADS, HEAD_W),
            stack("cp"), stack("hp"),
            ks_all.reshape(depth, n_seq, 1, N_HEADS, HEAD_W), vs_all.reshape(depth, n_seq, 1, N_HEADS, HEAD_W),
            stack("cs"), stack("hs"))
```

```python
import functools
import math

import jax
import jax.numpy as jnp
from jax import lax
from jax.experimental import pallas as pl
from jax.experimental.pallas import tpu as pltpu

F32 = jnp.float32
BF16 = jnp.bfloat16

LANES = 128
N_HEADS = 4
HEAD_DIM = 64
HEAD_W = 2 * HEAD_DIM
CONV_W = 4
LRU_C = 8.0
LN_EPS = 1e-5
NEG_INF = -1e30
QK_SCALE = HEAD_DIM ** -0.5

VMEM_LIMIT_BYTES = 56 * 1024 * 1024


def _compiler_params(semantics):
    return pltpu.CompilerParams(dimension_semantics=semantics, vmem_limit_bytes=VMEM_LIMIT_BYTES)


def _resident(shape):
    return pl.BlockSpec(shape, lambda *_: (0,) * len(shape), pipeline_mode=pl.Buffered(1))


def _layer_weight(shape, layer):
    return pl.BlockSpec((None,) + shape, lambda *_: (layer,) + (0,) * len(shape), pipeline_mode=pl.Buffered(1))


def _layer_norm(y, g, b):
    mu = jnp.mean(y, axis=-1, keepdims=True)
    d = y - mu
    var = jnp.mean(d * d, axis=-1, keepdims=True)
    return d * lax.rsqrt(var + LN_EPS) * g + b


def _dot(a, b):
    return jnp.dot(a, b, preferred_element_type=F32)


def _swiglu_ln(x, wg_ref, wu_ref, wd_ref, g_ref, b_ref, alpha):
    xb = x.astype(BF16)
    gate = _dot(xb, wg_ref[...])
    up = _dot(xb, wu_ref[...])
    h = (gate * jax.nn.sigmoid(gate) * up).astype(BF16)
    return _layer_norm(alpha * x + 0.5 * _dot(h, wd_ref[...]), g_ref[...], b_ref[...])


def _ffn_proj_kernel(x_ref, wg_ref, wu_ref, wd_ref, g_ref, b_ref, w_in_ref, k_all_in, v_all_in,
                     x1_ref, q_ref, k_all_ref, v_all_ref, kb_ref, vb_ref, xr_ref, xg_ref, *, alpha, d_attn, d_rec):
    del k_all_in, v_all_in
    x1 = _swiglu_ln(x_ref[...], wg_ref, wu_ref, wd_ref, g_ref, b_ref, alpha)
    x1_ref[...] = x1
    proj = _dot(x1.astype(BF16), w_in_ref[...])
    q = proj[:, :d_attn]
    k = proj[:, d_attn:2 * d_attn]
    v = proj[:, 2 * d_attn:3 * d_attn]
    q_ref[...] = (q * QK_SCALE).astype(BF16)
    for h in range(N_HEADS):
        k_all_ref[:, h, :] = k[:, h * HEAD_W:(h + 1) * HEAD_W]
        v_all_ref[:, h, :] = v[:, h * HEAD_W:(h + 1) * HEAD_W]
    kb_ref[...] = k.astype(BF16)
    vb_ref[...] = v.astype(BF16)
    xr_ref[...] = proj[:, 3 * d_attn:3 * d_attn + d_rec]
    xg_ref[...] = proj[:, 3 * d_attn + d_rec:]


def _ffn_proj(x, wg, wu, wd, g, b, w_in, k_all, v_all, *, layer, alpha, d_attn, d_rec, tm):
    n, d = x.shape
    d_ff = wg.shape[-1]
    d_in = w_in.shape[-1]
    weight = functools.partial(_layer_weight, layer=layer)
    row = lambda w: pl.BlockSpec((tm, w), lambda i: (i, 0))
    sds = lambda w, dt: jax.ShapeDtypeStruct((n, w), dt)
    layer_rows = pl.BlockSpec((None, tm, N_HEADS, HEAD_W), lambda i: (layer, i, 0, 0))
    in_hbm = pl.BlockSpec(memory_space=pl.ANY)
    return pl.pallas_call(
        functools.partial(_ffn_proj_kernel, alpha=alpha, d_attn=d_attn, d_rec=d_rec),
        grid=(n // tm,),
        in_specs=[row(d), weight((d, d_ff)), weight((d, d_ff)), weight((d_ff, d)),
                  _resident((1, d)), _resident((1, d)), weight((d, d_in)), in_hbm, in_hbm],
        out_specs=[row(d), row(d_attn), layer_rows, layer_rows, row(d_attn), row(d_attn), row(d_rec), row(d_rec)],
        out_shape=[sds(d, F32), sds(d_attn, BF16), jax.ShapeDtypeStruct(k_all.shape, F32),
                   jax.ShapeDtypeStruct(v_all.shape, F32), sds(d_attn, BF16), sds(d_attn, BF16),
                   sds(d_rec, F32), sds(d_rec, F32)],
        input_output_aliases={7: 2, 8: 3},
        compiler_params=_compiler_params(("parallel",)),
        name="ffn_proj",
    )(x, wg, wu, wd, g, b, w_in, k_all, v_all)


def _mix_ffn_kernel(x_ref, attn_ref, rec_ref, w_out_ref, g2_ref, b2_ref, wg_ref, wu_ref, wd_ref, g3_ref, b3_ref,
                    o_ref, *, alpha):
    mixed = jnp.concatenate([attn_ref[...], rec_ref[...]], axis=-1)
    x2 = _layer_norm(alpha * x_ref[...] + _dot(mixed, w_out_ref[...]), g2_ref[...], b2_ref[...])
    o_ref[...] = _swiglu_ln(x2, wg_ref, wu_ref, wd_ref, g3_ref, b3_ref, alpha)


def _mix_ffn(x, attn, rec, w_out, g2, b2, wg, wu, wd, g3, b3, *, layer, alpha, tm):
    n, d = x.shape
    d_ff = wg.shape[-1]
    weight = functools.partial(_layer_weight, layer=layer)
    row = lambda w: pl.BlockSpec((tm, w), lambda i: (i, 0))
    vec_spec = _resident((1, d))
    return pl.pallas_call(
        functools.partial(_mix_ffn_kernel, alpha=alpha),
        grid=(n // tm,),
        in_specs=[row(d), row(attn.shape[1]), row(rec.shape[1]), weight((d, d)), vec_spec, vec_spec,
                  weight((d, d_ff)), weight((d, d_ff)), weight((d_ff, d)), vec_spec, vec_spec],
        out_specs=row(d),
        out_shape=jax.ShapeDtypeStruct((n, d), F32),
        compiler_params=_compiler_params(("parallel",)),
        name="mix_ffn",
    )(x, attn, rec, w_out, g2, b2, wg, wu, wd, g3, b3)


def _lambda(lq1_ref, lk1_ref, lq2_ref, lk2_ref, lam_init):
    s1 = jnp.sum(lq1_ref[...] * lk1_ref[...], axis=-1, keepdims=True)
    s2 = jnp.sum(lq2_ref[...] * lk2_ref[...], axis=-1, keepdims=True)
    return jnp.exp(s1) - jnp.exp(s2) + lam_init


def _sub_ln(o, g, lam_init):
    ms = jnp.mean(o * o, axis=-1, keepdims=True)
    return o * lax.rsqrt(ms + LN_EPS) * g * (1.0 - lam_init)


def _stack_maps(q):
    lane = lax.broadcasted_iota(jnp.int32, q.shape, 1)
    zero = jnp.zeros_like(q)
    return jnp.concatenate([jnp.where(lane < HEAD_DIM, q, zero), jnp.where(lane >= HEAD_DIM, q, zero)], axis=0)


def _prompt_attn_kernel(lq1_ref, lk1_ref, lq2_ref, lk2_ref, g_ref, q_ref, k_ref, v_ref, o_ref, vx_ref,
                        *, lam_init, tq, nq, n_heads):
    qi = pl.program_id(2)

    @pl.when(qi == 0)
    def _():
        for h in range(n_heads):
            vx_ref[h, :, :HEAD_W] = v_ref[:, h * HEAD_W:(h + 1) * HEAD_W]
            vx_ref[h, :, HEAD_W:] = jnp.ones((vx_ref.shape[1], HEAD_W), BF16)

    lam = _lambda(lq1_ref, lk1_ref, lq2_ref, lk2_ref, lam_init)
    row = lax.broadcasted_iota(jnp.int32, (2 * tq, tq), 0)
    col = lax.broadcasted_iota(jnp.int32, (2 * tq, tq), 1)
    causal = col <= jnp.where(row >= tq, row - tq, row)

    def attend_head(h, n_keys):
        lanes = slice(h * HEAD_W, (h + 1) * HEAD_W)
        qs = _stack_maps(q_ref[:, lanes])
        s = lax.dot_general(qs, k_ref[:n_keys, lanes], (((1,), (1,)), ((), ())), preferred_element_type=F32)
        s_diag = jnp.where(causal, s[:, n_keys - tq:], NEG_INF)
        chunks = [s[:, c * LANES:(c + 1) * LANES] for c in range((n_keys - tq) // LANES)]
        chunks += [s_diag[:, c * LANES:(c + 1) * LANES] for c in range(tq // LANES)]
        m = jnp.max(functools.reduce(jnp.maximum, chunks), axis=-1, keepdims=True)
        m_b = jnp.broadcast_to(m, (2 * tq, LANES))
        p = jnp.concatenate([jnp.exp(c - m_b) for c in chunks], axis=-1).astype(BF16)
        acc = _dot(p, vx_ref[h, :n_keys, :])
        o_all = acc[:, :HEAD_W] / acc[:, HEAD_W:]
        o = o_all[:tq] - lam * o_all[tq:]
        return _sub_ln(o, g_ref[...], lam_init).astype(o_ref.dtype)

    def attend(n_keys):
        o_ref[...] = jnp.concatenate([attend_head(h, n_keys) for h in range(n_heads)], axis=-1)

    for i in range(nq):
        pl.when(qi == i)(functools.partial(attend, (i + 1) * tq))


def _prompt_attn(q, kb, vb, lq1, lk1, lq2, lk2, g, *, lam_init, batch, seq, tq, heads_per_step):
    n, d_attn = q.shape
    nq = seq // tq
    width = heads_per_step * HEAD_W
    small = lambda a: _resident(a.shape)
    return pl.pallas_call(
        functools.partial(_prompt_attn_kernel, lam_init=lam_init, tq=tq, nq=nq, n_heads=heads_per_step),
        grid=(batch, N_HEADS // heads_per_step, nq),
        in_specs=[
            small(lq1), small(lk1), small(lq2), small(lk2), small(g),
            pl.BlockSpec((tq, width), lambda b, h, i: (b * nq + i, h)),
            pl.BlockSpec((seq, width), lambda b, h, i: (b, h)),
            pl.BlockSpec((seq, width), lambda b, h, i: (b, h)),
        ],
        out_specs=pl.BlockSpec((tq, width), lambda b, h, i: (b * nq + i, h)),
        out_shape=jax.ShapeDtypeStruct((n, d_attn), BF16),
        scratch_shapes=[pltpu.VMEM((heads_per_step, seq, 2 * HEAD_W), BF16)],
        compiler_params=_compiler_params(("parallel", "parallel", "arbitrary")),
        name="prompt_attn",
    )(lq1, lk1, lq2, lk2, g, q, kb, vb)


def _decode_attn_kernel(pt_ref, lq1_ref, lk1_ref, lq2_ref, lk2_ref, g_ref, q_ref, kn_ref, vn_ref, *rest,
                        lam_init, pages_per_step):
    del pt_ref
    k_refs = rest[:pages_per_step]
    v_refs = rest[pages_per_step:2 * pages_per_step]
    o_ref, m_ref, l_ref, acc_ref = rest[2 * pages_per_step:]
    step = pl.program_id(1)

    qm_f32 = _stack_maps(q_ref[...].astype(F32))
    qm = qm_f32.astype(BF16)

    @pl.when(step == 0)
    def _():
        m_ref[...] = jnp.full(m_ref.shape, NEG_INF, F32)
        l_ref[...] = jnp.zeros(l_ref.shape, F32)
        acc_ref[...] = jnp.zeros(acc_ref.shape, F32)

    def merge(m, l, acc):
        m_prev = m_ref[...]
        m_new = jnp.maximum(m_prev, m)
        corr = jnp.exp(m_prev - m_new)
        w = jnp.exp(m - m_new)
        l_ref[...] = corr * l_ref[...] + w * l
        acc_ref[...] = corr * acc_ref[...] + w * acc
        m_ref[...] = m_new

    k_cat = jnp.concatenate([r[...].astype(BF16) for r in k_refs], axis=0)
    v_cat = jnp.concatenate([r[...].astype(BF16) for r in v_refs], axis=0)
    s = lax.dot_general(qm, k_cat, (((1,), (1,)), ((), ())), preferred_element_type=F32)
    row = lax.broadcasted_iota(jnp.int32, s.shape, 0)
    col = lax.broadcasted_iota(jnp.int32, s.shape, 1)
    s = jnp.where(col % N_HEADS == row % N_HEADS, s, NEG_INF)
    m = jnp.max(s, axis=-1, keepdims=True)
    p = jnp.exp(s - m)
    merge(m, jnp.sum(p, axis=-1, keepdims=True), _dot(p.astype(BF16), v_cat))

    @pl.when(step == pl.num_programs(1) - 1)
    def _():
        kn = kn_ref[...].astype(BF16).astype(F32)
        vn = vn_ref[...].astype(BF16).astype(F32)
        s_self = jnp.sum(qm_f32 * jnp.concatenate([kn, kn], axis=0), axis=-1, keepdims=True)
        merge(s_self, jnp.ones_like(s_self), jnp.concatenate([vn, vn], axis=0))

        lam = _lambda(lq1_ref, lk1_ref, lq2_ref, lk2_ref, lam_init)
        o_all = acc_ref[...] / l_ref[...]
        o = o_all[:N_HEADS] - lam * o_all[N_HEADS:]
        o_ref[...] = _sub_ln(o, g_ref[...], lam_init).astype(o_ref.dtype)


def _decode_attn(q, k_new, v_new, cache_k, cache_v, page_table, lq1, lk1, lq2, lk2, g, *,
                 layer, lam_init, pages_per_step):
    n_seq, n_pages = page_table.shape
    page_rows = cache_k.shape[2]
    n_steps = n_pages // pages_per_step
    small = lambda a: pl.BlockSpec(a.shape, lambda s, p, pt: (0,) * a.ndim)
    per_seq = pl.BlockSpec((None, N_HEADS, HEAD_W), lambda s, p, pt: (s, 0, 0))

    def page_spec(j):
        return pl.BlockSpec((None, None, page_rows, HEAD_W),
                            lambda s, p, pt: (layer, pt[s, p * pages_per_step + j], 0, 0))

    pages = [page_spec(j) for j in range(pages_per_step)]
    return pl.pallas_call(
        functools.partial(_decode_attn_kernel, lam_init=lam_init, pages_per_step=pages_per_step),
        grid_spec=pltpu.PrefetchScalarGridSpec(
            num_scalar_prefetch=1,
            grid=(n_seq, n_steps),
            in_specs=[small(lq1), small(lk1), small(lq2), small(lk2), small(g), per_seq, per_seq, per_seq]
                     + pages + pages,
            out_specs=per_seq,
            scratch_shapes=[pltpu.VMEM((2 * N_HEADS, 1), F32), pltpu.VMEM((2 * N_HEADS, 1), F32),
                            pltpu.VMEM((2 * N_HEADS, HEAD_W), F32)],
        ),
        out_shape=jax.ShapeDtypeStruct((n_seq, N_HEADS, HEAD_W), BF16),
        compiler_params=_compiler_params(("parallel", "arbitrary")),
        name="decode_attn",
    )(page_table, lq1, lk1, lq2, lk2, g, q, k_new, v_new,
      *([cache_k] * pages_per_step), *([cache_v] * pages_per_step))


def _gelu_tanh(x):
    return 0.5 * x * (1.0 + jnp.tanh(math.sqrt(2.0 / math.pi) * (x + 0.044715 * (x * x * x))))


def _lru_coefficients(xc, wa_ref, ba_ref, wx_ref, bx_ref, lam_ref):
    xcb = xc.astype(BF16)
    r = jax.nn.sigmoid(_dot(xcb, wa_ref[...]) + ba_ref[...])
    i = jax.nn.sigmoid(_dot(xcb, wx_ref[...]) + bx_ref[...])
    log_a = -LRU_C * r * jax.nn.softplus(-lam_ref[...])
    a = jnp.exp(log_a)
    b = jnp.sqrt(-jnp.tanh(log_a) * (a * a + 1.0)) * i * xc
    return a, b


def _rglru_seq_kernel(xr_ref, xg_ref, conv0_ref, h0_ref, cw_ref, cb_ref, wa_ref, ba_ref, wx_ref, bx_ref, lam_ref,
                      y_ref, convn_ref, hn_ref, xs_ref, a_ref, b_ref, h_ref, *, tt, nb):
    t = pl.program_id(0)
    d_rec = xr_ref.shape[-1]
    hist = CONV_W - 1

    @pl.when(t == 0)
    def _():
        xs_ref[:hist] = conv0_ref[...]
        h_ref[...] = h0_ref[...]

    @pl.when(t > 0)
    def _():
        xs_ref[:hist] = xs_ref[tt:tt + hist]

    xs_ref[hist:] = jnp.swapaxes(xr_ref[...], 0, 1)
    xc = cb_ref[...][None]
    for j in range(CONV_W):
        xc = xc + xs_ref[j:j + tt] * cw_ref[j:j + 1, :][None]

    a, b = _lru_coefficients(xc.reshape(tt * nb, d_rec), wa_ref, ba_ref, wx_ref, bx_ref, lam_ref)
    a_ref[...] = a.reshape(tt, nb, d_rec)
    b_ref[...] = b.reshape(tt, nb, d_rec)

    def step(s, h):
        h = a_ref[s] * h + b_ref[s]
        b_ref[s] = h
        return h

    h = lax.fori_loop(0, tt, step, h_ref[...], unroll=8)
    h_ref[...] = h

    y = b_ref[...] * _gelu_tanh(jnp.swapaxes(xg_ref[...], 0, 1))
    y_ref[...] = jnp.swapaxes(y, 0, 1).astype(y_ref.dtype)
    convn_ref[...] = xs_ref[tt:tt + hist]
    hn_ref[...] = h


def _rglru_seq(xr, xg, conv0, h0, cw, cb, wa, ba, wx, bx, lam, *, tt):
    nb, seq, d_rec = xr.shape
    tile = pl.BlockSpec((nb, tt, d_rec), lambda t: (0, t, 0))
    whole = lambda a: pl.BlockSpec(a.shape, lambda t: (0,) * a.ndim)
    return pl.pallas_call(
        functools.partial(_rglru_seq_kernel, tt=tt, nb=nb),
        grid=(seq // tt,),
        in_specs=[tile, tile] + [whole(a) for a in (conv0, h0, cw, cb, wa, ba, wx, bx, lam)],
        out_specs=[tile, whole(conv0), whole(h0)],
        out_shape=[jax.ShapeDtypeStruct((nb, seq, d_rec), BF16),
                   jax.ShapeDtypeStruct(conv0.shape, F32), jax.ShapeDtypeStruct(h0.shape, F32)],
        scratch_shapes=[pltpu.VMEM((CONV_W - 1 + tt, nb, d_rec), F32), pltpu.VMEM((tt, nb, d_rec), F32),
                        pltpu.VMEM((tt, nb, d_rec), F32), pltpu.VMEM((nb, d_rec), F32)],
        compiler_params=_compiler_params(("arbitrary",)),
        name="rglru_seq",
    )(xr, xg, conv0, h0, cw, cb, wa, ba, wx, bx, lam)


def _rglru_step_kernel(xr_ref, xg_ref, conv_ref, h0_ref, cw_ref, cb_ref, wa_ref, ba_ref, wx_ref, bx_ref, lam_ref,
                       y_ref, convn_ref, hn_ref):
    xr = xr_ref[...]
    xc = cb_ref[...] + xr * cw_ref[CONV_W - 1:CONV_W, :]
    for j in range(CONV_W - 1):
        xc = xc + conv_ref[j] * cw_ref[j:j + 1, :]
    a, b = _lru_coefficients(xc, wa_ref, ba_ref, wx_ref, bx_ref, lam_ref)
    h = a * h0_ref[...] + b
    y_ref[...] = (h * _gelu_tanh(xg_ref[...])).astype(y_ref.dtype)
    for j in range(CONV_W - 2):
        convn_ref[j] = conv_ref[j + 1]
    convn_ref[CONV_W - 2] = xr
    hn_ref[...] = h


def _rglru_step(xr, xg, conv_t, h0, cw, cb, wa, ba, wx, bx, lam):
    args = (xr, xg, conv_t, h0, cw, cb, wa, ba, wx, bx, lam)
    whole = lambda a: pl.BlockSpec(a.shape, lambda: (0,) * a.ndim)
    return pl.pallas_call(
        _rglru_step_kernel,
        in_specs=[whole(a) for a in args],
        out_specs=[whole(xr), whole(conv_t), whole(h0)],
        out_shape=[jax.ShapeDtypeStruct(xr.shape, BF16), jax.ShapeDtypeStruct(conv_t.shape, F32),
                   jax.ShapeDtypeStruct(h0.shape, F32)],
        name="rglru_step",
    )(*args)


def _block_diag(w):
    n_blk, blk, _ = w.shape
    eye = jnp.eye(n_blk, dtype=w.dtype)
    return jnp.einsum("nde,nm->ndme", w, eye).reshape(n_blk * blk, n_blk * blk)


def _row_tile(n, preferred):
    return preferred if n % preferred == 0 else n


def kernel(x_prompt, x_sample, cache_k, cache_v, state_conv, state_h, page_table, ln1_g, ln1_b, ln2_g, ln2_b, ln3_g, ln3_b, ffn1_w_gate, ffn1_w_up, ffn1_w_down, ffn2_w_gate, ffn2_w_up, ffn2_w_down, w_in, w_out, attn_lam_q1, attn_lam_k1, attn_lam_q2, attn_lam_k2, attn_subln_g, conv_w, conv_b, gate_a_w, gate_a_b, gate_x_w, gate_x_b, lru_lambda):
    depth = w_in.shape[0]
    batch, seq, d_model = x_prompt.shape
    n_seq = x_sample.shape[0]
    assert x_sample.shape[1] == 1, "the sample group decodes one token per sequence"
    d_attn = N_HEADS * HEAD_W
    d_rec = d_model - d_attn
    alpha = (2 * depth) ** 0.25
    n_pool, page = cache_k.shape[1:3]
    n_tok = batch * seq

    cache_k2 = cache_k.reshape(depth, n_pool, page * N_HEADS, HEAD_W)
    cache_v2 = cache_v.reshape(depth, n_pool, page * N_HEADS, HEAD_W)
    bf = lambda w: w.astype(BF16)
    vec = lambda p, l: p[l][None, :]

    xp = x_prompt.reshape(n_tok, d_model)
    xs = x_sample.reshape(n_seq, d_model)
    conv_p = jnp.zeros((CONV_W - 1, batch, d_rec), F32)
    h_p = jnp.zeros((batch, d_rec), F32)
    kp_all = jnp.zeros((depth, n_tok, N_HEADS, HEAD_W), F32)
    vp_all = jnp.zeros((depth, n_tok, N_HEADS, HEAD_W), F32)
    ks_all = jnp.zeros((depth, n_seq, N_HEADS, HEAD_W), F32)
    vs_all = jnp.zeros((depth, n_seq, N_HEADS, HEAD_W), F32)

    tm = _row_tile(n_tok, 512)
    tq = _row_tile(seq, 256)
    tt = _row_tile(seq, 256)
    pages_per_step = next(p for p in (32, 16, 8, 4, 2, 1) if page_table.shape[1] % p == 0)

    ffn1_w = (bf(ffn1_w_gate), bf(ffn1_w_up), bf(ffn1_w_down))
    ffn2_w = (bf(ffn2_w_gate), bf(ffn2_w_up), bf(ffn2_w_down))
    w_in_bf, w_out_bf = bf(w_in), bf(w_out)

    outs = {name: [] for name in ("cp", "hp", "cs", "hs")}
    for l in range(depth):
        lam_init = 0.8 - 0.6 * math.exp(-0.3 * l)
        ffn1 = (*ffn1_w, vec(ln1_g, l), vec(ln1_b, l))
        ffn2 = (*ffn2_w, vec(ln3_g, l), vec(ln3_b, l))
        lam_vecs = (vec(attn_lam_q1, l), vec(attn_lam_k1, l), vec(attn_lam_q2, l), vec(attn_lam_k2, l))
        sub_g = vec(attn_subln_g, l)
        rec = (conv_w[l], vec(conv_b, l), bf(_block_diag(gate_a_w[l])), vec(gate_a_b, l),
               bf(_block_diag(gate_x_w[l])), vec(gate_x_b, l), vec(lru_lambda, l))

        xp, q, kp_all, vp_all, kb, vb, xr, xg = _ffn_proj(xp, *ffn1, w_in_bf, kp_all, vp_all, layer=l, alpha=alpha,
                                                          d_attn=d_attn, d_rec=d_rec, tm=tm)
        attn = _prompt_attn(q, kb, vb, *lam_vecs, sub_g, lam_init=lam_init, batch=batch, seq=seq, tq=tq,
                            heads_per_step=N_HEADS)
        y, conv_n, h_n = _rglru_seq(xr.reshape(batch, seq, d_rec), xg.reshape(batch, seq, d_rec),
                                    conv_p, h_p, *rec, tt=tt)
        xp = _mix_ffn(xp, attn, y.reshape(n_tok, d_rec), w_out_bf, vec(ln2_g, l), vec(ln2_b, l), *ffn2,
                      layer=l, alpha=alpha, tm=tm)
        outs["cp"].append(jnp.swapaxes(conv_n, 0, 1))
        outs["hp"].append(h_n)

        xs, q, ks_all, vs_all, _, _, xr, xg = _ffn_proj(xs, *ffn1, w_in_bf, ks_all, vs_all, layer=l, alpha=alpha,
                                                        d_attn=d_attn, d_rec=d_rec, tm=n_seq)
        attn = _decode_attn(q.reshape(n_seq, N_HEADS, HEAD_W), ks_all[l], vs_all[l], cache_k2, cache_v2, page_table,
                            *lam_vecs, sub_g, layer=l, lam_init=lam_init, pages_per_step=pages_per_step)
        y, conv_n, h_n = _rglru_step(xr, xg, jnp.swapaxes(state_conv[l], 0, 1), state_h[l], *rec)
        xs = _mix_ffn(xs, attn.reshape(n_seq, d_attn), y, w_out_bf, vec(ln2_g, l), vec(ln2_b, l), *ffn2,
                      layer=l, alpha=alpha, tm=n_seq)
        outs["cs"].append(jnp.swapaxes(conv_n, 0, 1))
        outs["hs"].append(h_n)

    stack = lambda name: jnp.stack(outs[name])
    return (xp.reshape(batch, seq, d_model), xs.reshape(n_seq, 1, d_model),
            kp_all.reshape(depth, batch, seq, N_HEADS, HEAD_W), vp_all.reshape(depth, batch, seq, N_HEADS, HEAD_W),
            stack("cp"), stack("hp"),
            ks_all.reshape(depth, n_seq, 1, N_HEADS, HEAD_W), vs_all.reshape(depth, n_seq, 1, N_HEADS, HEAD_W),
            stack("cs"), stack("hs"))
```

```python
import functools
import math

import jax
import jax.numpy as jnp
from jax import lax
from jax.experimental import pallas as pl
from jax.experimental.pallas import tpu as pltpu

F32 = jnp.float32
BF16 = jnp.bfloat16

LANES = 128
N_HEADS = 4
HEAD_DIM = 64
HEAD_W = 2 * HEAD_DIM
CONV_W = 4
LRU_C = 8.0
LN_EPS = 1e-5
NEG_INF = -1e30
QK_SCALE = HEAD_DIM ** -0.5

ROW_GROUP = 256

VMEM_LIMIT_BYTES = 56 * 1024 * 1024


def _compiler_params(semantics):
    return pltpu.CompilerParams(dimension_semantics=semantics, vmem_limit_bytes=VMEM_LIMIT_BYTES)


def _resident(shape):
    return pl.BlockSpec(shape, lambda *_: (0,) * len(shape), pipeline_mode=pl.Buffered(1))


def _layer_weight(shape, layer):
    return pl.BlockSpec((None,) + shape, lambda *_: (layer,) + (0,) * len(shape), pipeline_mode=pl.Buffered(1))


def _layer_norm(y, g, b):
    mu = jnp.mean(y, axis=-1, keepdims=True)
    d = y - mu
    var = jnp.mean(d * d, axis=-1, keepdims=True)
    return d * lax.rsqrt(var + LN_EPS) * g + b


def _dot(a, b):
    return jnp.dot(a, b, preferred_element_type=F32)


def _swiglu_ln(xs, wg_ref, wu_ref, wd_ref, g_ref, b_ref, alpha):
    xbs = [x.astype(BF16) for x in xs]
    gates = [_dot(xb, wg_ref[...]) for xb in xbs]
    ups = [_dot(xb, wu_ref[...]) for xb in xbs]
    hs = [(gate * jax.nn.sigmoid(gate) * up).astype(BF16) for gate, up in zip(gates, ups)]
    fs = [_dot(h, wd_ref[...]) for h in hs]
    return [_layer_norm(alpha * x + 0.5 * f, g_ref[...], b_ref[...]) for x, f in zip(xs, fs)]


def _row_groups(n_rows):
    size = ROW_GROUP if n_rows % ROW_GROUP == 0 else n_rows
    return [slice(r, r + size) for r in range(0, n_rows, size)]


def _ffn_proj_kernel(x_ref, wg_ref, wu_ref, wd_ref, g_ref, b_ref, w_in_ref, k_all_in, v_all_in,
                     x1_ref, q_ref, k_all_ref, v_all_ref, kb_ref, vb_ref, xr_ref, xg_ref, *, alpha, d_attn, d_rec):
    del k_all_in, v_all_in
    groups = _row_groups(x_ref.shape[0])
    x1s = _swiglu_ln([x_ref[rows, :] for rows in groups], wg_ref, wu_ref, wd_ref, g_ref, b_ref, alpha)
    projs = [_dot(x1.astype(BF16), w_in_ref[...]) for x1 in x1s]
    for rows, x1, proj in zip(groups, x1s, projs):
        x1_ref[rows, :] = x1
        q = proj[:, :d_attn]
        k = proj[:, d_attn:2 * d_attn]
        v = proj[:, 2 * d_attn:3 * d_attn]
        q_ref[rows, :] = (q * QK_SCALE).astype(BF16)
        for h in range(N_HEADS):
            k_all_ref[rows, h, :] = k[:, h * HEAD_W:(h + 1) * HEAD_W]
            v_all_ref[rows, h, :] = v[:, h * HEAD_W:(h + 1) * HEAD_W]
        kb_ref[rows, :] = k.astype(BF16)
        vb_ref[rows, :] = v.astype(BF16)
        xr_ref[rows, :] = proj[:, 3 * d_attn:3 * d_attn + d_rec]
        xg_ref[rows, :] = proj[:, 3 * d_attn + d_rec:]


def _ffn_proj(x, wg, wu, wd, g, b, w_in, k_all, v_all, *, layer, alpha, d_attn, d_rec, tm):
    n, d = x.shape
    d_ff = wg.shape[-1]
    d_in = w_in.shape[-1]
    weight = functools.partial(_layer_weight, layer=layer)
    row = lambda w: pl.BlockSpec((tm, w), lambda i: (i, 0))
    sds = lambda w, dt: jax.ShapeDtypeStruct((n, w), dt)
    layer_rows = pl.BlockSpec((None, tm, N_HEADS, HEAD_W), lambda i: (layer, i, 0, 0))
    in_hbm = pl.BlockSpec(memory_space=pl.ANY)
    return pl.pallas_call(
        functools.partial(_ffn_proj_kernel, alpha=alpha, d_attn=d_attn, d_rec=d_rec),
        grid=(n // tm,),
        in_specs=[row(d), weight((d, d_ff)), weight((d, d_ff)), weight((d_ff, d)),
                  _resident((1, d)), _resident((1, d)), weight((d, d_in)), in_hbm, in_hbm],
        out_specs=[row(d), row(d_attn), layer_rows, layer_rows, row(d_attn), row(d_attn), row(d_rec), row(d_rec)],
        out_shape=[sds(d, F32), sds(d_attn, BF16), jax.ShapeDtypeStruct(k_all.shape, F32),
                   jax.ShapeDtypeStruct(v_all.shape, F32), sds(d_attn, BF16), sds(d_attn, BF16),
                   sds(d_rec, F32), sds(d_rec, F32)],
        input_output_aliases={7: 2, 8: 3},
        compiler_params=_compiler_params(("parallel",)),
        name="ffn_proj",
    )(x, wg, wu, wd, g, b, w_in, k_all, v_all)


def _mix_ffn_kernel(x_ref, attn_ref, rec_ref, w_out_ref, g2_ref, b2_ref, wg_ref, wu_ref, wd_ref, g3_ref, b3_ref,
                    o_ref, *, alpha):
    groups = _row_groups(x_ref.shape[0])
    mixed = [jnp.concatenate([attn_ref[rows, :], rec_ref[rows, :]], axis=-1) for rows in groups]
    mixes = [_dot(m, w_out_ref[...]) for m in mixed]
    x2 = [_layer_norm(alpha * x_ref[rows, :] + mix, g2_ref[...], b2_ref[...]) for rows, mix in zip(groups, mixes)]
    for rows, out in zip(groups, _swiglu_ln(x2, wg_ref, wu_ref, wd_ref, g3_ref, b3_ref, alpha)):
        o_ref[rows, :] = out


def _mix_ffn(x, attn, rec, w_out, g2, b2, wg, wu, wd, g3, b3, *, layer, alpha, tm):
    n, d = x.shape
    d_ff = wg.shape[-1]
    weight = functools.partial(_layer_weight, layer=layer)
    row = lambda w: pl.BlockSpec((tm, w), lambda i: (i, 0))
    vec_spec = _resident((1, d))
    return pl.pallas_call(
        functools.partial(_mix_ffn_kernel, alpha=alpha),
        grid=(n // tm,),
        in_specs=[row(d), row(attn.shape[1]), row(rec.shape[1]), weight((d, d)), vec_spec, vec_spec,
                  weight((d, d_ff)), weight((d, d_ff)), weight((d_ff, d)), vec_spec, vec_spec],
        out_specs=row(d),
        out_shape=jax.ShapeDtypeStruct((n, d), F32),
        compiler_params=_compiler_params(("parallel",)),
        name="mix_ffn",
    )(x, attn, rec, w_out, g2, b2, wg, wu, wd, g3, b3)


def _lambda(lq1_ref, lk1_ref, lq2_ref, lk2_ref, lam_init):
    s1 = jnp.sum(lq1_ref[...] * lk1_ref[...], axis=-1, keepdims=True)
    s2 = jnp.sum(lq2_ref[...] * lk2_ref[...], axis=-1, keepdims=True)
    return jnp.exp(s1) - jnp.exp(s2) + lam_init


def _sub_ln(o, g, lam_init):
    ms = jnp.mean(o * o, axis=-1, keepdims=True)
    return o * lax.rsqrt(ms + LN_EPS) * g * (1.0 - lam_init)


def _stack_maps(q):
    lane = lax.broadcasted_iota(jnp.int32, q.shape, 1)
    zero = jnp.zeros_like(q)
    return jnp.concatenate([jnp.where(lane < HEAD_DIM, q, zero), jnp.where(lane >= HEAD_DIM, q, zero)], axis=0)


def _prompt_attn_kernel(lq1_ref, lk1_ref, lq2_ref, lk2_ref, g_ref, q_ref, k_ref, v_ref, o_ref, vx_ref,
                        *, lam_init, tq, nq, n_heads):
    qi = pl.program_id(2)

    @pl.when(qi == 0)
    def _():
        for h in range(n_heads):
            vx_ref[h, :, :HEAD_W] = v_ref[:, h * HEAD_W:(h + 1) * HEAD_W]
            vx_ref[h, :, HEAD_W:] = jnp.ones((vx_ref.shape[1], HEAD_W), BF16)

    lam = _lambda(lq1_ref, lk1_ref, lq2_ref, lk2_ref, lam_init)
    row = lax.broadcasted_iota(jnp.int32, (2 * tq, tq), 0)
    col = lax.broadcasted_iota(jnp.int32, (2 * tq, tq), 1)
    causal = col <= jnp.where(row >= tq, row - tq, row)

    def lane_chunks(s, n_keys):
        s_diag = jnp.where(causal, s[:, n_keys - tq:], NEG_INF)
        chunks = [s[:, c * LANES:(c + 1) * LANES] for c in range((n_keys - tq) // LANES)]
        return chunks + [s_diag[:, c * LANES:(c + 1) * LANES] for c in range(tq // LANES)]

    def row_max(chunks):
        m = jnp.max(functools.reduce(jnp.maximum, chunks), axis=-1, keepdims=True)
        return jnp.broadcast_to(m, (2 * tq, LANES))

    def finish(acc):
        o_all = acc[:, :HEAD_W] / acc[:, HEAD_W:]
        o = o_all[:tq] - lam * o_all[tq:]
        return _sub_ln(o, g_ref[...], lam_init).astype(o_ref.dtype)

    def attend(n_keys):
        lanes = [slice(h * HEAD_W, (h + 1) * HEAD_W) for h in range(n_heads)]
        qs = [_stack_maps(q_ref[:, ln]) for ln in lanes]
        scores = [lax.dot_general(q, k_ref[:n_keys, ln], (((1,), (1,)), ((), ())), preferred_element_type=F32)
                  for q, ln in zip(qs, lanes)]
        chunks = [lane_chunks(s, n_keys) for s in scores]
        maxes = [row_max(c) for c in chunks]
        ps = [jnp.concatenate([jnp.exp(c - m_b) for c in cs], axis=-1).astype(BF16) for cs, m_b in zip(chunks, maxes)]
        accs = [_dot(p, vx_ref[h, :n_keys, :]) for h, p in enumerate(ps)]
        o_ref[...] = jnp.concatenate([finish(acc) for acc in accs], axis=-1)

    for i in range(nq):
        pl.when(qi == i)(functools.partial(attend, (i + 1) * tq))


def _prompt_attn(q, kb, vb, lq1, lk1, lq2, lk2, g, *, lam_init, batch, seq, tq, heads_per_step):
    n, d_attn = q.shape
    nq = seq // tq
    width = heads_per_step * HEAD_W
    small = lambda a: _resident(a.shape)
    return pl.pallas_call(
        functools.partial(_prompt_attn_kernel, lam_init=lam_init, tq=tq, nq=nq, n_heads=heads_per_step),
        grid=(batch, N_HEADS // heads_per_step, nq),
        in_specs=[
            small(lq1), small(lk1), small(lq2), small(lk2), small(g),
            pl.BlockSpec((tq, width), lambda b, h, i: (b * nq + i, h)),
            pl.BlockSpec((seq, width), lambda b, h, i: (b, h)),
            pl.BlockSpec((seq, width), lambda b, h, i: (b, h)),
        ],
        out_specs=pl.BlockSpec((tq, width), lambda b, h, i: (b * nq + i, h)),
        out_shape=jax.ShapeDtypeStruct((n, d_attn), BF16),
        scratch_shapes=[pltpu.VMEM((heads_per_step, seq, 2 * HEAD_W), BF16)],
        compiler_params=_compiler_params(("parallel", "parallel", "arbitrary")),
        name="prompt_attn",
    )(lq1, lk1, lq2, lk2, g, q, kb, vb)


def _decode_attn_kernel(pt_ref, lq1_ref, lk1_ref, lq2_ref, lk2_ref, g_ref, q_ref, kn_ref, vn_ref, *rest,
                        lam_init, pages_per_step):
    del pt_ref
    k_refs = rest[:pages_per_step]
    v_refs = rest[pages_per_step:2 * pages_per_step]
    o_ref, m_ref, l_ref, acc_ref = rest[2 * pages_per_step:]
    step = pl.program_id(1)

    qm_f32 = _stack_maps(q_ref[...].astype(F32))
    qm = qm_f32.astype(BF16)

    @pl.when(step == 0)
    def _():
        m_ref[...] = jnp.full(m_ref.shape, NEG_INF, F32)
        l_ref[...] = jnp.zeros(l_ref.shape, F32)
        acc_ref[...] = jnp.zeros(acc_ref.shape, F32)

    def merge(m, l, acc):
        m_prev = m_ref[...]
        m_new = jnp.maximum(m_prev, m)
        corr = jnp.exp(m_prev - m_new)
        w = jnp.exp(m - m_new)
        l_ref[...] = corr * l_ref[...] + w * l
        acc_ref[...] = corr * acc_ref[...] + w * acc
        m_ref[...] = m_new

    k_cat = jnp.concatenate([r[...].astype(BF16) for r in k_refs], axis=0)
    v_cat = jnp.concatenate([r[...].astype(BF16) for r in v_refs], axis=0)
    s = lax.dot_general(qm, k_cat, (((1,), (1,)), ((), ())), preferred_element_type=F32)
    row = lax.broadcasted_iota(jnp.int32, s.shape, 0)
    col = lax.broadcasted_iota(jnp.int32, s.shape, 1)
    s = jnp.where(col % N_HEADS == row % N_HEADS, s, NEG_INF)
    m = jnp.max(s, axis=-1, keepdims=True)
    p = jnp.exp(s - m)
    merge(m, jnp.sum(p, axis=-1, keepdims=True), _dot(p.astype(BF16), v_cat))

    @pl.when(step == pl.num_programs(1) - 1)
    def _():
        kn = kn_ref[...].astype(BF16).astype(F32)
        vn = vn_ref[...].astype(BF16).astype(F32)
        s_self = jnp.sum(qm_f32 * jnp.concatenate([kn, kn], axis=0), axis=-1, keepdims=True)
        merge(s_self, jnp.ones_like(s_self), jnp.concatenate([vn, vn], axis=0))

        lam = _lambda(lq1_ref, lk1_ref, lq2_ref, lk2_ref, lam_init)
        o_all = acc_ref[...] / l_ref[...]
        o = o_all[:N_HEADS] - lam * o_all[N_HEADS:]
        o_ref[...] = _sub_ln(o, g_ref[...], lam_init).astype(o_ref.dtype)


def _decode_attn(q, k_new, v_new, cache_k, cache_v, page_table, lq1, lk1, lq2, lk2, g, *,
                 layer, lam_init, pages_per_step):
    n_seq, n_pages = page_table.shape
    page_rows = cache_k.shape[2]
    n_steps = n_pages // pages_per_step
    small = lambda a: pl.BlockSpec(a.shape, lambda s, p, pt: (0,) * a.ndim)
    per_seq = pl.BlockSpec((None, N_HEADS, HEAD_W), lambda s, p, pt: (s, 0, 0))

    def page_spec(j):
        return pl.BlockSpec((None, None, page_rows, HEAD_W),
                            lambda s, p, pt: (layer, pt[s, p * pages_per_step + j], 0, 0))

    pages = [page_spec(j) for j in range(pages_per_step)]
    return pl.pallas_call(
        functools.partial(_decode_attn_kernel, lam_init=lam_init, pages_per_step=pages_per_step),
        grid_spec=pltpu.PrefetchScalarGridSpec(
            num_scalar_prefetch=1,
            grid=(n_seq, n_steps),
            in_specs=[small(lq1), small(lk1), small(lq2), small(lk2), small(g), per_seq, per_seq, per_seq]
                     + pages + pages,
            out_specs=per_seq,
            scratch_shapes=[pltpu.VMEM((2 * N_HEADS, 1), F32), pltpu.VMEM((2 * N_HEADS, 1), F32),
                            pltpu.VMEM((2 * N_HEADS, HEAD_W), F32)],
        ),
        out_shape=jax.ShapeDtypeStruct((n_seq, N_HEADS, HEAD_W), BF16),
        compiler_params=_compiler_params(("parallel", "arbitrary")),
        name="decode_attn",
    )(page_table, lq1, lk1, lq2, lk2, g, q, k_new, v_new,
      *([cache_k] * pages_per_step), *([cache_v] * pages_per_step))


def _gelu_tanh(x):
    return 0.5 * x * (1.0 + jnp.tanh(math.sqrt(2.0 / math.pi) * (x + 0.044715 * (x * x * x))))


def _lru_coefficients(xc, wa_ref, ba_ref, wx_ref, bx_ref, lam_ref):
    xcb = xc.astype(BF16)
    r = jax.nn.sigmoid(_dot(xcb, wa_ref[...]) + ba_ref[...])
    i = jax.nn.sigmoid(_dot(xcb, wx_ref[...]) + bx_ref[...])
    log_a = -LRU_C * r * jax.nn.softplus(-lam_ref[...])
    a = jnp.exp(log_a)
    b = jnp.sqrt(-jnp.tanh(log_a) * (a * a + 1.0)) * i * xc
    return a, b


def _rglru_seq_kernel(xr_ref, xg_ref, conv0_ref, h0_ref, cw_ref, cb_ref, wa_ref, ba_ref, wx_ref, bx_ref, lam_ref,
                      y_ref, convn_ref, hn_ref, xs_ref, a_ref, b_ref, h_ref, *, tt, nb):
    t = pl.program_id(0)
    d_rec = xr_ref.shape[-1]
    hist = CONV_W - 1

    @pl.when(t == 0)
    def _():
        xs_ref[:hist] = conv0_ref[...]
        h_ref[...] = h0_ref[...]

    @pl.when(t > 0)
    def _():
        xs_ref[:hist] = xs_ref[tt:tt + hist]

    xs_ref[hist:] = jnp.swapaxes(xr_ref[...], 0, 1)
    xc = cb_ref[...][None]
    for j in range(CONV_W):
        xc = xc + xs_ref[j:j + tt] * cw_ref[j:j + 1, :][None]

    a, b = _lru_coefficients(xc.reshape(tt * nb, d_rec), wa_ref, ba_ref, wx_ref, bx_ref, lam_ref)
    a_ref[...] = a.reshape(tt, nb, d_rec)
    b_ref[...] = b.reshape(tt, nb, d_rec)

    def step(s, h):
        h = a_ref[s] * h + b_ref[s]
        b_ref[s] = h
        return h

    h = lax.fori_loop(0, tt, step, h_ref[...], unroll=8)
    h_ref[...] = h

    y = b_ref[...] * _gelu_tanh(jnp.swapaxes(xg_ref[...], 0, 1))
    y_ref[...] = jnp.swapaxes(y, 0, 1).astype(y_ref.dtype)
    convn_ref[...] = xs_ref[tt:tt + hist]
    hn_ref[...] = h


def _rglru_seq(xr, xg, conv0, h0, cw, cb, wa, ba, wx, bx, lam, *, tt):
    nb, seq, d_rec = xr.shape
    tile = pl.BlockSpec((nb, tt, d_rec), lambda t: (0, t, 0))
    whole = lambda a: pl.BlockSpec(a.shape, lambda t: (0,) * a.ndim)
    return pl.pallas_call(
        functools.partial(_rglru_seq_kernel, tt=tt, nb=nb),
        grid=(seq // tt,),
        in_specs=[tile, tile] + [whole(a) for a in (conv0, h0, cw, cb, wa, ba, wx, bx, lam)],
        out_specs=[tile, whole(conv0), whole(h0)],
        out_shape=[jax.ShapeDtypeStruct((nb, seq, d_rec), BF16),
                   jax.ShapeDtypeStruct(conv0.shape, F32), jax.ShapeDtypeStruct(h0.shape, F32)],
        scratch_shapes=[pltpu.VMEM((CONV_W - 1 + tt, nb, d_rec), F32), pltpu.VMEM((tt, nb, d_rec), F32),
                        pltpu.VMEM((tt, nb, d_rec), F32), pltpu.VMEM((nb, d_rec), F32)],
        compiler_params=_compiler_params(("arbitrary",)),
        name="rglru_seq",
    )(xr, xg, conv0, h0, cw, cb, wa, ba, wx, bx, lam)


def _rglru_step_kernel(xr_ref, xg_ref, conv_ref, h0_ref, cw_ref, cb_ref, wa_ref, ba_ref, wx_ref, bx_ref, lam_ref,
                       y_ref, convn_ref, hn_ref):
    xr = xr_ref[...]
    xc = cb_ref[...] + xr * cw_ref[CONV_W - 1:CONV_W, :]
    for j in range(CONV_W - 1):
        xc = xc + conv_ref[j] * cw_ref[j:j + 1, :]
    a, b = _lru_coefficients(xc, wa_ref, ba_ref, wx_ref, bx_ref, lam_ref)
    h = a * h0_ref[...] + b
    y_ref[...] = (h * _gelu_tanh(xg_ref[...])).astype(y_ref.dtype)
    for j in range(CONV_W - 2):
        convn_ref[j] = conv_ref[j + 1]
    convn_ref[CONV_W - 2] = xr
    hn_ref[...] = h


def _rglru_step(xr, xg, conv_t, h0, cw, cb, wa, ba, wx, bx, lam):
    args = (xr, xg, conv_t, h0, cw, cb, wa, ba, wx, bx, lam)
    whole = lambda a: pl.BlockSpec(a.shape, lambda: (0,) * a.ndim)
    return pl.pallas_call(
        _rglru_step_kernel,
        in_specs=[whole(a) for a in args],
        out_specs=[whole(xr), whole(conv_t), whole(h0)],
        out_shape=[jax.ShapeDtypeStruct(xr.shape, BF16), jax.ShapeDtypeStruct(conv_t.shape, F32),
                   jax.ShapeDtypeStruct(h0.shape, F32)],
        name="rglru_step",
    )(*args)


def _block_diag(w):
    n_blk, blk, _ = w.shape
    eye = jnp.eye(n_blk, dtype=w.dtype)
    return jnp.einsum("nde,nm->ndme", w, eye).reshape(n_blk * blk, n_blk * blk)


def _row_tile(n, preferred):
    return preferred if n % preferred == 0 else n


def kernel(x_prompt, x_sample, cache_k, cache_v, state_conv, state_h, page_table, ln1_g, ln1_b, ln2_g, ln2_b, ln3_g, ln3_b, ffn1_w_gate, ffn1_w_up, ffn1_w_down, ffn2_w_gate, ffn2_w_up, ffn2_w_down, w_in, w_out, attn_lam_q1, attn_lam_k1, attn_lam_q2, attn_lam_k2, attn_subln_g, conv_w, conv_b, gate_a_w, gate_a_b, gate_x_w, gate_x_b, lru_lambda):
    depth = w_in.shape[0]
    batch, seq, d_model = x_prompt.shape
    n_seq = x_sample.shape[0]
    assert x_sample.shape[1] == 1, "the sample group decodes one token per sequence"
    d_attn = N_HEADS * HEAD_W
    d_rec = d_model - d_attn
    alpha = (2 * depth) ** 0.25
    n_pool, page = cache_k.shape[1:3]
    n_tok = batch * seq

    cache_k2 = cache_k.reshape(depth, n_pool, page * N_HEADS, HEAD_W)
    cache_v2 = cache_v.reshape(depth, n_pool, page * N_HEADS, HEAD_W)
    bf = lambda w: w.astype(BF16)
    vec = lambda p, l: p[l][None, :]

    xp = x_prompt.reshape(n_tok, d_model)
    xs = x_sample.reshape(n_seq, d_model)
    conv_p = jnp.zeros((CONV_W - 1, batch, d_rec), F32)
    h_p = jnp.zeros((batch, d_rec), F32)
    kp_all = jnp.zeros((depth, n_tok, N_HEADS, HEAD_W), F32)
    vp_all = jnp.zeros((depth, n_tok, N_HEADS, HEAD_W), F32)
    ks_all = jnp.zeros((depth, n_seq, N_HEADS, HEAD_W), F32)
    vs_all = jnp.zeros((depth, n_seq, N_HEADS, HEAD_W), F32)

    tm = _row_tile(n_tok, 512)
    tq = _row_tile(seq, 256)
    tt = _row_tile(seq, 256)
    pages_per_step = next(p for p in (32, 16, 8, 4, 2, 1) if page_table.shape[1] % p == 0)

    ffn1_w = (bf(ffn1_w_gate), bf(ffn1_w_up), bf(ffn1_w_down))
    ffn2_w = (bf(ffn2_w_gate), bf(ffn2_w_up), bf(ffn2_w_down))
    w_in_bf, w_out_bf = bf(w_in), bf(w_out)

    outs = {name: [] for name in ("cp", "hp", "cs", "hs")}
    for l in range(depth):
        lam_init = 0.8 - 0.6 * math.exp(-0.3 * l)
        ffn1 = (*ffn1_w, vec(ln1_g, l), vec(ln1_b, l))
        ffn2 = (*ffn2_w, vec(ln3_g, l), vec(ln3_b, l))
        lam_vecs = (vec(attn_lam_q1, l), vec(attn_lam_k1, l), vec(attn_lam_q2, l), vec(attn_lam_k2, l))
        sub_g = vec(attn_subln_g, l)
        rec = (conv_w[l], vec(conv_b, l), bf(_block_diag(gate_a_w[l])), vec(gate_a_b, l),
               bf(_block_diag(gate_x_w[l])), vec(gate_x_b, l), vec(lru_lambda, l))

        xp, q, kp_all, vp_all, kb, vb, xr, xg = _ffn_proj(xp, *ffn1, w_in_bf, kp_all, vp_all, layer=l, alpha=alpha,
                                                          d_attn=d_attn, d_rec=d_rec, tm=tm)
        attn = _prompt_attn(q, kb, vb, *lam_vecs, sub_g, lam_init=lam_init, batch=batch, seq=seq, tq=tq,
                            heads_per_step=N_HEADS)
        y, conv_n, h_n = _rglru_seq(xr.reshape(batch, seq, d_rec), xg.reshape(batch, seq, d_rec),
                                    conv_p, h_p, *rec, tt=tt)
        xp = _mix_ffn(xp, attn, y.reshape(n_tok, d_rec), w_out_bf, vec(ln2_g, l), vec(ln2_b, l), *ffn2,
                      layer=l, alpha=alpha, tm=tm)
        outs["cp"].append(jnp.swapaxes(conv_n, 0, 1))
        outs["hp"].append(h_n)

        xs, q, ks_all, vs_all, _, _, xr, xg = _ffn_proj(xs, *ffn1, w_in_bf, ks_all, vs_all, layer=l, alpha=alpha,
                                                        d_attn=d_attn, d_rec=d_rec, tm=n_seq)
        attn = _decode_attn(q.reshape(n_seq, N_HEADS, HEAD_W), ks_all[l], vs_all[l], cache_k2, cache_v2, page_table,
                            *lam_vecs, sub_g, layer=l, lam_init=lam_init, pages_per_step=pages_per_step)
        y, conv_n, h_n = _rglru_step(xr, xg, jnp.swapaxes(state_conv[l], 0, 1), state_h[l], *rec)
        xs = _mix_ffn(xs, attn.reshape(n_seq, d_attn), y, w_out_bf, vec(ln2_g, l), vec(ln2_b, l), *ffn2,
                      layer=l, alpha=alpha, tm=n_seq)
        outs["cs"].append(jnp.swapaxes(conv_n, 0, 1))
        outs["hs"].append(h_n)

    stack = lambda name: jnp.stack(outs[name])
    return (xp.reshape(batch, seq, d_model), xs.reshape(n_seq, 1, d_model),
            kp_all.reshape(depth, batch, seq, N_HEADS, HEAD_W), vp_all.reshape(depth, batch, seq, N_HEADS, HEAD_W),
            stack("cp"), stack("hp"),
            ks_all.reshape(depth, n_seq, 1, N_HEADS, HEAD_W), vs_all.reshape(depth, n_seq, 1, N_HEADS, HEAD_W),
            stack("cs"), stack("hs"))
```

```python
import functools
import math

import jax
import jax.numpy as jnp
from jax import lax
from jax.experimental import pallas as pl
from jax.experimental.pallas import tpu as pltpu

F32 = jnp.float32
BF16 = jnp.bfloat16

LANES = 128
BF16_SUBLANES = 16
N_HEADS = 4
HEAD_DIM = 64
HEAD_W = 2 * HEAD_DIM
CONV_W = 4
LRU_C = 8.0
LN_EPS = 1e-5
NEG_INF = -1e30
QK_SCALE = HEAD_DIM ** -0.5

ROW_GROUP = 256

VMEM_LIMIT_BYTES = 56 * 1024 * 1024


def _compiler_params(semantics):
    return pltpu.CompilerParams(dimension_semantics=semantics, vmem_limit_bytes=VMEM_LIMIT_BYTES)


def _resident(shape):
    return pl.BlockSpec(shape, lambda *_: (0,) * len(shape), pipeline_mode=pl.Buffered(1))


def _layer_weight(shape, layer):
    return pl.BlockSpec((None,) + shape, lambda *_: (layer,) + (0,) * len(shape), pipeline_mode=pl.Buffered(1))


def _layer_norm(y, g, b):
    mu = jnp.mean(y, axis=-1, keepdims=True)
    d = y - mu
    var = jnp.mean(d * d, axis=-1, keepdims=True)
    return d * lax.rsqrt(var + LN_EPS) * g + b


def _dot(a, b):
    return jnp.dot(a, b, preferred_element_type=F32)


def _swiglu_ln(xs, wg_ref, wu_ref, wd_ref, g_ref, b_ref, alpha):
    xbs = [x.astype(BF16) for x in xs]
    gates = [_dot(xb, wg_ref[...]) for xb in xbs]
    ups = [_dot(xb, wu_ref[...]) for xb in xbs]
    hs = [(gate * jax.nn.sigmoid(gate) * up).astype(BF16) for gate, up in zip(gates, ups)]
    fs = [_dot(h, wd_ref[...]) for h in hs]
    return [_layer_norm(alpha * x + 0.5 * f, g_ref[...], b_ref[...]) for x, f in zip(xs, fs)]


def _cast_slabs(w, layer, n_steps):
    _, n_rows, n_cols = w.shape
    hold = next(k for k in range(1, n_steps + 1)
                if (n_rows * k) % n_steps == 0 and (n_rows * k // n_steps) % BF16_SUBLANES == 0)
    slab = n_rows * hold // n_steps
    src = pl.BlockSpec((None, slab, n_cols), lambda i: (layer, i // hold, 0))
    dst = pl.BlockSpec((slab, n_cols), lambda i: (i // hold, 0))
    return src, dst


def _cast_weights(src_refs, dst_refs):
    for src, dst in zip(src_refs, dst_refs):
        dst[...] = src[...].astype(BF16)


def _row_groups(n_rows):
    size = ROW_GROUP if n_rows % ROW_GROUP == 0 else n_rows
    return [slice(r, r + size) for r in range(0, n_rows, size)]


def _ffn_proj_kernel(x_ref, wg_ref, wu_ref, wd_ref, g_ref, b_ref, w_in_ref, k_all_in, v_all_in, *rest,
                     alpha, d_attn, d_rec, n_cast):
    del k_all_in, v_all_in
    cast_src, rest = rest[:n_cast], rest[n_cast:]
    x1_ref, q_ref, k_all_ref, v_all_ref, kb_ref, vb_ref, xr_ref, xg_ref = rest[:8]
    _cast_weights(cast_src, rest[8:])
    groups = _row_groups(x_ref.shape[0])
    x1s = _swiglu_ln([x_ref[rows, :] for rows in groups], wg_ref, wu_ref, wd_ref, g_ref, b_ref, alpha)
    projs = [_dot(x1.astype(BF16), w_in_ref[...]) for x1 in x1s]
    for rows, x1, proj in zip(groups, x1s, projs):
        x1_ref[rows, :] = x1
        q = proj[:, :d_attn]
        k = proj[:, d_attn:2 * d_attn]
        v = proj[:, 2 * d_attn:3 * d_attn]
        q_ref[rows, :] = (q * QK_SCALE).astype(BF16)
        for h in range(N_HEADS):
            k_all_ref[rows, h, :] = k[:, h * HEAD_W:(h + 1) * HEAD_W]
            v_all_ref[rows, h, :] = v[:, h * HEAD_W:(h + 1) * HEAD_W]
        kb_ref[rows, :] = k.astype(BF16)
        vb_ref[rows, :] = v.astype(BF16)
        xr_ref[rows, :] = proj[:, 3 * d_attn:3 * d_attn + d_rec]
        xg_ref[rows, :] = proj[:, 3 * d_attn + d_rec:]


def _ffn_proj(x, wg, wu, wd, g, b, w_in, k_all, v_all, *, layer, alpha, d_attn, d_rec, tm, cast_src=()):
    n, d = x.shape
    slabs = [_cast_slabs(w, layer, n // tm) for w in cast_src]
    d_ff = wg.shape[-1]
    d_in = w_in.shape[-1]
    weight = functools.partial(_layer_weight, layer=layer)
    row = lambda w: pl.BlockSpec((tm, w), lambda i: (i, 0))
    sds = lambda w, dt: jax.ShapeDtypeStruct((n, w), dt)
    layer_rows = pl.BlockSpec((None, tm, N_HEADS, HEAD_W), lambda i: (layer, i, 0, 0))
    in_hbm = pl.BlockSpec(memory_space=pl.ANY)
    return pl.pallas_call(
        functools.partial(_ffn_proj_kernel, alpha=alpha, d_attn=d_attn, d_rec=d_rec, n_cast=len(cast_src)),
        grid=(n // tm,),
        in_specs=[row(d), _resident((d, d_ff)), _resident((d, d_ff)), _resident((d_ff, d)),
                  _resident((1, d)), _resident((1, d)), weight((d, d_in)), in_hbm, in_hbm]
                 + [src for src, _ in slabs],
        out_specs=[row(d), row(d_attn), layer_rows, layer_rows, row(d_attn), row(d_attn), row(d_rec), row(d_rec)]
                  + [dst for _, dst in slabs],
        out_shape=[sds(d, F32), sds(d_attn, BF16), jax.ShapeDtypeStruct(k_all.shape, F32),
                   jax.ShapeDtypeStruct(v_all.shape, F32), sds(d_attn, BF16), sds(d_attn, BF16),
                   sds(d_rec, F32), sds(d_rec, F32)]
                  + [jax.ShapeDtypeStruct(w.shape[1:], BF16) for w in cast_src],
        input_output_aliases={7: 2, 8: 3},
        compiler_params=_compiler_params(("arbitrary",)),
        name="ffn_proj",
    )(x, wg, wu, wd, g, b, w_in, k_all, v_all, *cast_src)


def _mix_ffn_kernel(x_ref, attn_ref, rec_ref, w_out_ref, g2_ref, b2_ref, wg_ref, wu_ref, wd_ref, g3_ref, b3_ref,
                    *rest, alpha, n_cast):
    o_ref = rest[n_cast]
    _cast_weights(rest[:n_cast], rest[n_cast + 1:])
    groups = _row_groups(x_ref.shape[0])
    mixed = [jnp.concatenate([attn_ref[rows, :], rec_ref[rows, :]], axis=-1) for rows in groups]
    mixes = [_dot(m, w_out_ref[...]) for m in mixed]
    x2 = [_layer_norm(alpha * x_ref[rows, :] + mix, g2_ref[...], b2_ref[...]) for rows, mix in zip(groups, mixes)]
    for rows, out in zip(groups, _swiglu_ln(x2, wg_ref, wu_ref, wd_ref, g3_ref, b3_ref, alpha)):
        o_ref[rows, :] = out


def _mix_ffn(x, attn, rec, w_out, g2, b2, wg, wu, wd, g3, b3, *, layer, alpha, tm, cast_src=(), cast_layer=0):
    n, d = x.shape
    slabs = [_cast_slabs(w, cast_layer, n // tm) for w in cast_src]
    d_ff = wg.shape[-1]
    weight = functools.partial(_layer_weight, layer=layer)
    row = lambda w: pl.BlockSpec((tm, w), lambda i: (i, 0))
    vec_spec = _resident((1, d))
    return pl.pallas_call(
        functools.partial(_mix_ffn_kernel, alpha=alpha, n_cast=len(cast_src)),
        grid=(n // tm,),
        in_specs=[row(d), row(attn.shape[1]), row(rec.shape[1]), weight((d, d)), vec_spec, vec_spec,
                  _resident((d, d_ff)), _resident((d, d_ff)), _resident((d_ff, d)), vec_spec, vec_spec]
                 + [src for src, _ in slabs],
        out_specs=[row(d)] + [dst for _, dst in slabs],
        out_shape=[jax.ShapeDtypeStruct((n, d), F32)] + [jax.ShapeDtypeStruct(w.shape[1:], BF16) for w in cast_src],
        compiler_params=_compiler_params(("arbitrary",)),
        name="mix_ffn",
    )(x, attn, rec, w_out, g2, b2, wg, wu, wd, g3, b3, *cast_src)


def _lambda(lq1_ref, lk1_ref, lq2_ref, lk2_ref, lam_init):
    s1 = jnp.sum(lq1_ref[...] * lk1_ref[...], axis=-1, keepdims=True)
    s2 = jnp.sum(lq2_ref[...] * lk2_ref[...], axis=-1, keepdims=True)
    return jnp.exp(s1) - jnp.exp(s2) + lam_init


def _sub_ln(o, g, lam_init):
    ms = jnp.mean(o * o, axis=-1, keepdims=True)
    return o * lax.rsqrt(ms + LN_EPS) * g * (1.0 - lam_init)


def _stack_maps(q):
    lane = lax.broadcasted_iota(jnp.int32, q.shape, 1)
    zero = jnp.zeros_like(q)
    return jnp.concatenate([jnp.where(lane < HEAD_DIM, q, zero), jnp.where(lane >= HEAD_DIM, q, zero)], axis=0)


def _prompt_attn_kernel(lq1_ref, lk1_ref, lq2_ref, lk2_ref, g_ref, q_ref, k_ref, v_ref, o_ref, vx_ref,
                        *, lam_init, tq, nq, n_heads):
    qi = pl.program_id(2)

    @pl.when(qi == 0)
    def _():
        for h in range(n_heads):
            vx_ref[h, :, :HEAD_W] = v_ref[:, h * HEAD_W:(h + 1) * HEAD_W]
            vx_ref[h, :, HEAD_W:] = jnp.ones((vx_ref.shape[1], HEAD_W), BF16)

    lam = _lambda(lq1_ref, lk1_ref, lq2_ref, lk2_ref, lam_init)
    row = lax.broadcasted_iota(jnp.int32, (2 * tq, tq), 0)
    col = lax.broadcasted_iota(jnp.int32, (2 * tq, tq), 1)
    causal = col <= jnp.where(row >= tq, row - tq, row)

    def lane_chunks(s, n_keys):
        s_diag = jnp.where(causal, s[:, n_keys - tq:], NEG_INF)
        chunks = [s[:, c * LANES:(c + 1) * LANES] for c in range((n_keys - tq) // LANES)]
        return chunks + [s_diag[:, c * LANES:(c + 1) * LANES] for c in range(tq // LANES)]

    def row_max(chunks):
        m = jnp.max(functools.reduce(jnp.maximum, chunks), axis=-1, keepdims=True)
        return jnp.broadcast_to(m, (2 * tq, LANES))

    def finish(acc):
        o_all = acc[:, :HEAD_W] / acc[:, HEAD_W:]
        o = o_all[:tq] - lam * o_all[tq:]
        return _sub_ln(o, g_ref[...], lam_init).astype(o_ref.dtype)

    def attend(n_keys):
        lanes = [slice(h * HEAD_W, (h + 1) * HEAD_W) for h in range(n_heads)]
        qs = [_stack_maps(q_ref[:, ln]) for ln in lanes]
        scores = [lax.dot_general(q, k_ref[:n_keys, ln], (((1,), (1,)), ((), ())), preferred_element_type=F32)
                  for q, ln in zip(qs, lanes)]
        chunks = [lane_chunks(s, n_keys) for s in scores]
        maxes = [row_max(c) for c in chunks]
        ps = [jnp.concatenate([jnp.exp(c - m_b) for c in cs], axis=-1).astype(BF16) for cs, m_b in zip(chunks, maxes)]
        accs = [_dot(p, vx_ref[h, :n_keys, :]) for h, p in enumerate(ps)]
        o_ref[...] = jnp.concatenate([finish(acc) for acc in accs], axis=-1)

    for i in range(nq):
        pl.when(qi == i)(functools.partial(attend, (i + 1) * tq))


def _prompt_attn(q, kb, vb, lq1, lk1, lq2, lk2, g, *, lam_init, batch, seq, tq, heads_per_step):
    n, d_attn = q.shape
    nq = seq // tq
    width = heads_per_step * HEAD_W
    small = lambda a: _resident(a.shape)
    return pl.pallas_call(
        functools.partial(_prompt_attn_kernel, lam_init=lam_init, tq=tq, nq=nq, n_heads=heads_per_step),
        grid=(batch, N_HEADS // heads_per_step, nq),
        in_specs=[
            small(lq1), small(lk1), small(lq2), small(lk2), small(g),
            pl.BlockSpec((tq, width), lambda b, h, i: (b * nq + i, h)),
            pl.BlockSpec((seq, width), lambda b, h, i: (b, h)),
            pl.BlockSpec((seq, width), lambda b, h, i: (b, h)),
        ],
        out_specs=pl.BlockSpec((tq, width), lambda b, h, i: (b * nq + i, h)),
        out_shape=jax.ShapeDtypeStruct((n, d_attn), BF16),
        scratch_shapes=[pltpu.VMEM((heads_per_step, seq, 2 * HEAD_W), BF16)],
        compiler_params=_compiler_params(("parallel", "parallel", "arbitrary")),
        name="prompt_attn",
    )(lq1, lk1, lq2, lk2, g, q, kb, vb)


def _decode_attn_kernel(pt_ref, lq1_ref, lk1_ref, lq2_ref, lk2_ref, g_ref, q_ref, kn_ref, vn_ref, *rest,
                        lam_init, pages_per_step):
    del pt_ref
    k_refs = rest[:pages_per_step]
    v_refs = rest[pages_per_step:2 * pages_per_step]
    o_ref, m_ref, l_ref, acc_ref = rest[2 * pages_per_step:]
    step = pl.program_id(1)

    qm_f32 = _stack_maps(q_ref[...].astype(F32))
    qm = qm_f32.astype(BF16)

    @pl.when(step == 0)
    def _():
        m_ref[...] = jnp.full(m_ref.shape, NEG_INF, F32)
        l_ref[...] = jnp.zeros(l_ref.shape, F32)
        acc_ref[...] = jnp.zeros(acc_ref.shape, F32)

    def merge(m, l, acc):
        m_prev = m_ref[...]
        m_new = jnp.maximum(m_prev, m)
        corr = jnp.exp(m_prev - m_new)
        w = jnp.exp(m - m_new)
        l_ref[...] = corr * l_ref[...] + w * l
        acc_ref[...] = corr * acc_ref[...] + w * acc
        m_ref[...] = m_new

    k_cat = jnp.concatenate([r[...].astype(BF16) for r in k_refs], axis=0)
    v_cat = jnp.concatenate([r[...].astype(BF16) for r in v_refs], axis=0)
    s = lax.dot_general(qm, k_cat, (((1,), (1,)), ((), ())), preferred_element_type=F32)
    row = lax.broadcasted_iota(jnp.int32, s.shape, 0)
    col = lax.broadcasted_iota(jnp.int32, s.shape, 1)
    s = jnp.where(col % N_HEADS == row % N_HEADS, s, NEG_INF)
    m = jnp.max(s, axis=-1, keepdims=True)
    p = jnp.exp(s - m)
    merge(m, jnp.sum(p, axis=-1, keepdims=True), _dot(p.astype(BF16), v_cat))

    @pl.when(step == pl.num_programs(1) - 1)
    def _():
        kn = kn_ref[...].astype(BF16).astype(F32)
        vn = vn_ref[...].astype(BF16).astype(F32)
        s_self = jnp.sum(qm_f32 * jnp.concatenate([kn, kn], axis=0), axis=-1, keepdims=True)
        merge(s_self, jnp.ones_like(s_self), jnp.concatenate([vn, vn], axis=0))

        lam = _lambda(lq1_ref, lk1_ref, lq2_ref, lk2_ref, lam_init)
        o_all = acc_ref[...] / l_ref[...]
        o = o_all[:N_HEADS] - lam * o_all[N_HEADS:]
        o_ref[...] = _sub_ln(o, g_ref[...], lam_init).astype(o_ref.dtype)


def _decode_attn(q, k_new, v_new, cache_k, cache_v, page_table, lq1, lk1, lq2, lk2, g, *,
                 layer, lam_init, pages_per_step):
    n_seq, n_pages = page_table.shape
    page_rows = cache_k.shape[2]
    n_steps = n_pages // pages_per_step
    small = lambda a: pl.BlockSpec(a.shape, lambda s, p, pt: (0,) * a.ndim)
    per_seq = pl.BlockSpec((None, N_HEADS, HEAD_W), lambda s, p, pt: (s, 0, 0))

    def page_spec(j):
        return pl.BlockSpec((None, None, page_rows, HEAD_W),
                            lambda s, p, pt: (layer, pt[s, p * pages_per_step + j], 0, 0))

    pages = [page_spec(j) for j in range(pages_per_step)]
    return pl.pallas_call(
        functools.partial(_decode_attn_kernel, lam_init=lam_init, pages_per_step=pages_per_step),
        grid_spec=pltpu.PrefetchScalarGridSpec(
            num_scalar_prefetch=1,
            grid=(n_seq, n_steps),
            in_specs=[small(lq1), small(lk1), small(lq2), small(lk2), small(g), per_seq, per_seq, per_seq]
                     + pages + pages,
            out_specs=per_seq,
            scratch_shapes=[pltpu.VMEM((2 * N_HEADS, 1), F32), pltpu.VMEM((2 * N_HEADS, 1), F32),
                            pltpu.VMEM((2 * N_HEADS, HEAD_W), F32)],
        ),
        out_shape=jax.ShapeDtypeStruct((n_seq, N_HEADS, HEAD_W), BF16),
        compiler_params=_compiler_params(("parallel", "arbitrary")),
        name="decode_attn",
    )(page_table, lq1, lk1, lq2, lk2, g, q, k_new, v_new,
      *([cache_k] * pages_per_step), *([cache_v] * pages_per_step))


def _gelu_tanh(x):
    return 0.5 * x * (1.0 + jnp.tanh(math.sqrt(2.0 / math.pi) * (x + 0.044715 * (x * x * x))))


def _lru_coefficients(xc, wa_ref, ba_ref, wx_ref, bx_ref, lam_ref):
    xcb = xc.astype(BF16)
    r = jax.nn.sigmoid(_dot(xcb, wa_ref[...]) + ba_ref[...])
    i = jax.nn.sigmoid(_dot(xcb, wx_ref[...]) + bx_ref[...])
    log_a = -LRU_C * r * jax.nn.softplus(-lam_ref[...])
    a = jnp.exp(log_a)
    b = jnp.sqrt(-jnp.tanh(log_a) * (a * a + 1.0)) * i * xc
    return a, b


def _rglru_seq_kernel(xr_ref, xg_ref, conv0_ref, h0_ref, cw_ref, cb_ref, wa_ref, ba_ref, wx_ref, bx_ref, lam_ref,
                      y_ref, convn_ref, hn_ref, xs_ref, a_ref, b_ref, h_ref, *, tt, nb):
    t = pl.program_id(0)
    d_rec = xr_ref.shape[-1]
    hist = CONV_W - 1

    @pl.when(t == 0)
    def _():
        xs_ref[:hist] = conv0_ref[...]
        h_ref[...] = h0_ref[...]

    @pl.when(t > 0)
    def _():
        xs_ref[:hist] = xs_ref[tt:tt + hist]

    xs_ref[hist:] = jnp.swapaxes(xr_ref[...], 0, 1)
    xc = cb_ref[...][None]
    for j in range(CONV_W):
        xc = xc + xs_ref[j:j + tt] * cw_ref[j:j + 1, :][None]

    a, b = _lru_coefficients(xc.reshape(tt * nb, d_rec), wa_ref, ba_ref, wx_ref, bx_ref, lam_ref)
    a_ref[...] = a.reshape(tt, nb, d_rec)
    b_ref[...] = b.reshape(tt, nb, d_rec)

    def step(s, h):
        h = a_ref[s] * h + b_ref[s]
        b_ref[s] = h
        return h

    h = lax.fori_loop(0, tt, step, h_ref[...], unroll=8)
    h_ref[...] = h

    y = b_ref[...] * _gelu_tanh(jnp.swapaxes(xg_ref[...], 0, 1))
    y_ref[...] = jnp.swapaxes(y, 0, 1).astype(y_ref.dtype)
    convn_ref[...] = xs_ref[tt:tt + hist]
    hn_ref[...] = h


def _rglru_seq(xr, xg, conv0, h0, cw, cb, wa, ba, wx, bx, lam, *, tt):
    nb, seq, d_rec = xr.shape
    tile = pl.BlockSpec((nb, tt, d_rec), lambda t: (0, t, 0))
    whole = lambda a: pl.BlockSpec(a.shape, lambda t: (0,) * a.ndim)
    return pl.pallas_call(
        functools.partial(_rglru_seq_kernel, tt=tt, nb=nb),
        grid=(seq // tt,),
        in_specs=[tile, tile] + [whole(a) for a in (conv0, h0, cw, cb, wa, ba, wx, bx, lam)],
        out_specs=[tile, whole(conv0), whole(h0)],
        out_shape=[jax.ShapeDtypeStruct((nb, seq, d_rec), BF16),
                   jax.ShapeDtypeStruct(conv0.shape, F32), jax.ShapeDtypeStruct(h0.shape, F32)],
        scratch_shapes=[pltpu.VMEM((CONV_W - 1 + tt, nb, d_rec), F32), pltpu.VMEM((tt, nb, d_rec), F32),
                        pltpu.VMEM((tt, nb, d_rec), F32), pltpu.VMEM((nb, d_rec), F32)],
        compiler_params=_compiler_params(("arbitrary",)),
        name="rglru_seq",
    )(xr, xg, conv0, h0, cw, cb, wa, ba, wx, bx, lam)


def _rglru_step_kernel(xr_ref, xg_ref, conv_ref, h0_ref, cw_ref, cb_ref, wa_ref, ba_ref, wx_ref, bx_ref, lam_ref,
                       y_ref, convn_ref, hn_ref):
    xr = xr_ref[...]
    xc = cb_ref[...] + xr * cw_ref[CONV_W - 1:CONV_W, :]
    for j in range(CONV_W - 1):
        xc = xc + conv_ref[j] * cw_ref[j:j + 1, :]
    a, b = _lru_coefficients(xc, wa_ref, ba_ref, wx_ref, bx_ref, lam_ref)
    h = a * h0_ref[...] + b
    y_ref[...] = (h * _gelu_tanh(xg_ref[...])).astype(y_ref.dtype)
    for j in range(CONV_W - 2):
        convn_ref[j] = conv_ref[j + 1]
    convn_ref[CONV_W - 2] = xr
    hn_ref[...] = h


def _rglru_step(xr, xg, conv_t, h0, cw, cb, wa, ba, wx, bx, lam):
    args = (xr, xg, conv_t, h0, cw, cb, wa, ba, wx, bx, lam)
    whole = lambda a: pl.BlockSpec(a.shape, lambda: (0,) * a.ndim)
    return pl.pallas_call(
        _rglru_step_kernel,
        in_specs=[whole(a) for a in args],
        out_specs=[whole(xr), whole(conv_t), whole(h0)],
        out_shape=[jax.ShapeDtypeStruct(xr.shape, BF16), jax.ShapeDtypeStruct(conv_t.shape, F32),
                   jax.ShapeDtypeStruct(h0.shape, F32)],
        name="rglru_step",
    )(*args)


def _block_diag(w):
    n_blk, blk, _ = w.shape
    eye = jnp.eye(n_blk, dtype=w.dtype)
    return jnp.einsum("nde,nm->ndme", w, eye).reshape(n_blk * blk, n_blk * blk)


def _row_tile(n, preferred):
    return preferred if n % preferred == 0 else n


def kernel(x_prompt, x_sample, cache_k, cache_v, state_conv, state_h, page_table, ln1_g, ln1_b, ln2_g, ln2_b, ln3_g, ln3_b, ffn1_w_gate, ffn1_w_up, ffn1_w_down, ffn2_w_gate, ffn2_w_up, ffn2_w_down, w_in, w_out, attn_lam_q1, attn_lam_k1, attn_lam_q2, attn_lam_k2, attn_subln_g, conv_w, conv_b, gate_a_w, gate_a_b, gate_x_w, gate_x_b, lru_lambda):
    depth = w_in.shape[0]
    batch, seq, d_model = x_prompt.shape
    n_seq = x_sample.shape[0]
    assert x_sample.shape[1] == 1, "the sample group decodes one token per sequence"
    d_attn = N_HEADS * HEAD_W
    d_rec = d_model - d_attn
    alpha = (2 * depth) ** 0.25
    n_pool, page = cache_k.shape[1:3]
    n_tok = batch * seq

    cache_k2 = cache_k.reshape(depth, n_pool, page * N_HEADS, HEAD_W)
    cache_v2 = cache_v.reshape(depth, n_pool, page * N_HEADS, HEAD_W)
    bf = lambda w: w.astype(BF16)
    vec = lambda p, l: p[l][None, :]

    xp = x_prompt.reshape(n_tok, d_model)
    xs = x_sample.reshape(n_seq, d_model)
    conv_p = jnp.zeros((CONV_W - 1, batch, d_rec), F32)
    h_p = jnp.zeros((batch, d_rec), F32)
    kp_all = jnp.zeros((depth, n_tok, N_HEADS, HEAD_W), F32)
    vp_all = jnp.zeros((depth, n_tok, N_HEADS, HEAD_W), F32)
    ks_all = jnp.zeros((depth, n_seq, N_HEADS, HEAD_W), F32)
    vs_all = jnp.zeros((depth, n_seq, N_HEADS, HEAD_W), F32)

    tm = _row_tile(n_tok, 512)
    tq = _row_tile(seq, 256)
    tt = _row_tile(seq, 256)
    pages_per_step = next(p for p in (32, 16, 8, 4, 2, 1) if page_table.shape[1] % p == 0)

    ffn1_src = (ffn1_w_gate, ffn1_w_up, ffn1_w_down)
    ffn2_src = (ffn2_w_gate, ffn2_w_up, ffn2_w_down)
    ffn1_w = tuple(bf(w[0]) for w in ffn1_src)
    w_in_bf, w_out_bf = bf(w_in), bf(w_out)

    outs = {name: [] for name in ("cp", "hp", "cs", "hs")}
    for l in range(depth):
        lam_init = 0.8 - 0.6 * math.exp(-0.3 * l)
        ffn1 = (*ffn1_w, vec(ln1_g, l), vec(ln1_b, l))
        lam_vecs = (vec(attn_lam_q1, l), vec(attn_lam_k1, l), vec(attn_lam_q2, l), vec(attn_lam_k2, l))
        sub_g = vec(attn_subln_g, l)
        rec = (conv_w[l], vec(conv_b, l), bf(_block_diag(gate_a_w[l])), vec(gate_a_b, l),
               bf(_block_diag(gate_x_w[l])), vec(gate_x_b, l), vec(lru_lambda, l))

        xp, q, kp_all, vp_all, kb, vb, xr, xg, *ffn2_w = _ffn_proj(
            xp, *ffn1, w_in_bf, kp_all, vp_all, layer=l, alpha=alpha, d_attn=d_attn, d_rec=d_rec, tm=tm,
            cast_src=ffn2_src)
        ffn2 = (*ffn2_w, vec(ln3_g, l), vec(ln3_b, l))
        attn = _prompt_attn(q, kb, vb, *lam_vecs, sub_g, lam_init=lam_init, batch=batch, seq=seq, tq=tq,
                            heads_per_step=N_HEADS)
        y, conv_n, h_n = _rglru_seq(xr.reshape(batch, seq, d_rec), xg.reshape(batch, seq, d_rec),
                                    conv_p, h_p, *rec, tt=tt)
        xp, *ffn1_next = _mix_ffn(xp, attn, y.reshape(n_tok, d_rec), w_out_bf, vec(ln2_g, l), vec(ln2_b, l), *ffn2,
                                  layer=l, alpha=alpha, tm=tm,
                                  cast_src=ffn1_src if l + 1 < depth else (), cast_layer=l + 1)
        outs["cp"].append(jnp.swapaxes(conv_n, 0, 1))
        outs["hp"].append(h_n)

        xs, q, ks_all, vs_all, _, _, xr, xg = _ffn_proj(xs, *ffn1, w_in_bf, ks_all, vs_all, layer=l, alpha=alpha,
                                                        d_attn=d_attn, d_rec=d_rec, tm=n_seq)
        attn = _decode_attn(q.reshape(n_seq, N_HEADS, HEAD_W), ks_all[l], vs_all[l], cache_k2, cache_v2, page_table,
                            *lam_vecs, sub_g, layer=l, lam_init=lam_init, pages_per_step=pages_per_step)
        y, conv_n, h_n = _rglru_step(xr, xg, jnp.swapaxes(state_conv[l], 0, 1), state_h[l], *rec)
        (xs,) = _mix_ffn(xs, attn.reshape(n_seq, d_attn), y, w_out_bf, vec(ln2_g, l), vec(ln2_b, l), *ffn2,
                         layer=l, alpha=alpha, tm=n_seq)
        outs["cs"].append(jnp.swapaxes(conv_n, 0, 1))
        outs["hs"].append(h_n)
        ffn1_w = tuple(ffn1_next) or ffn1_w

    stack = lambda name: jnp.stack(outs[name])
    return (xp.reshape(batch, seq, d_model), xs.reshape(n_seq, 1, d_model),
            kp_all.reshape(depth, batch, seq, N_HEADS, HEAD_W), vp_all.reshape(depth, batch, seq, N_HEADS, HEAD_W),
            stack("cp"), stack("hp"),
            ks_all.reshape(depth, n_seq, 1, N_HEADS, HEAD_W), vs_all.reshape(depth, n_seq, 1, N_HEADS, HEAD_W),
            stack("cs"), stack("hs"))
```

```python
import functools
import math

import jax
import jax.numpy as jnp
from jax import lax
from jax.experimental import pallas as pl
from jax.experimental.pallas import tpu as pltpu

F32 = jnp.float32
BF16 = jnp.bfloat16

LANES = 128
BF16_SUBLANES = 16
N_HEADS = 4
HEAD_DIM = 64
HEAD_W = 2 * HEAD_DIM
CONV_W = 4
LRU_C = 8.0
LN_EPS = 1e-5
NEG_INF = -1e30
QK_SCALE = HEAD_DIM ** -0.5

ROW_GROUP = 256

VMEM_LIMIT_BYTES = 56 * 1024 * 1024


def _compiler_params(semantics):
    return pltpu.CompilerParams(dimension_semantics=semantics, vmem_limit_bytes=VMEM_LIMIT_BYTES)


def _resident(shape):
    return pl.BlockSpec(shape, lambda *_: (0,) * len(shape), pipeline_mode=pl.Buffered(1))


def _layer_weight(shape, layer):
    return pl.BlockSpec((None,) + shape, lambda *_: (layer,) + (0,) * len(shape), pipeline_mode=pl.Buffered(1))


def _layer_norm(y, g, b):
    mu = jnp.mean(y, axis=-1, keepdims=True)
    d = y - mu
    var = jnp.mean(d * d, axis=-1, keepdims=True)
    return d * lax.rsqrt(var + LN_EPS) * g + b


def _dot(a, b):
    return jnp.dot(a, b, preferred_element_type=F32)


def _swiglu_ln(xs, wg_ref, wu_ref, wd_ref, g_ref, b_ref, alpha):
    xbs = [x.astype(BF16) for x in xs]
    gates = [_dot(xb, wg_ref[...]) for xb in xbs]
    ups = [_dot(xb, wu_ref[...]) for xb in xbs]
    hs = [(gate * jax.nn.sigmoid(gate) * up).astype(BF16) for gate, up in zip(gates, ups)]
    fs = [_dot(h, wd_ref[...]) for h in hs]
    return [_layer_norm(alpha * x + 0.5 * f, g_ref[...], b_ref[...]) for x, f in zip(xs, fs)]


def _cast_slabs(w, layer, n_steps):
    _, n_rows, n_cols = w.shape
    hold = next(k for k in range(1, n_steps + 1)
                if (n_rows * k) % n_steps == 0 and (n_rows * k // n_steps) % BF16_SUBLANES == 0)
    slab = n_rows * hold // n_steps
    src = pl.BlockSpec((None, slab, n_cols), lambda i: (layer, i // hold, 0))
    dst = pl.BlockSpec((slab, n_cols), lambda i: (i // hold, 0))
    return src, dst


def _cast_weights(src_refs, dst_refs):
    for src, dst in zip(src_refs, dst_refs):
        dst[...] = src[...].astype(BF16)


def _row_groups(n_rows):
    size = ROW_GROUP if n_rows % ROW_GROUP == 0 else n_rows
    return [slice(r, r + size) for r in range(0, n_rows, size)]


def _ffn_proj_kernel(x_ref, wg_ref, wu_ref, wd_ref, g_ref, b_ref, w_in_ref, k_all_in, v_all_in, *rest,
                     alpha, d_attn, d_rec, n_cast):
    del k_all_in, v_all_in
    cast_src, rest = rest[:n_cast], rest[n_cast:]
    x1_ref, q_ref, k_all_ref, v_all_ref, kb_ref, vb_ref, xr_ref, xg_ref = rest[:8]
    _cast_weights(cast_src, rest[8:])
    groups = _row_groups(x_ref.shape[0])
    x1s = _swiglu_ln([x_ref[rows, :] for rows in groups], wg_ref, wu_ref, wd_ref, g_ref, b_ref, alpha)
    projs = [_dot(x1.astype(BF16), w_in_ref[...]) for x1 in x1s]
    for rows, x1, proj in zip(groups, x1s, projs):
        x1_ref[rows, :] = x1
        q = proj[:, :d_attn]
        k = proj[:, d_attn:2 * d_attn]
        v = proj[:, 2 * d_attn:3 * d_attn]
        q_ref[rows, :] = (q * QK_SCALE).astype(BF16)
        k_all_ref[rows, :, :] = k.reshape(k.shape[0], N_HEADS, HEAD_W)
        v_all_ref[rows, :, :] = v.reshape(v.shape[0], N_HEADS, HEAD_W)
        kb_ref[rows, :] = k.astype(BF16)
        vb_ref[rows, :] = v.astype(BF16)
        xr_ref[rows, :] = proj[:, 3 * d_attn:3 * d_attn + d_rec]
        xg_ref[rows, :] = proj[:, 3 * d_attn + d_rec:]


def _ffn_proj(x, wg, wu, wd, g, b, w_in, k_all, v_all, *, layer, alpha, d_attn, d_rec, tm, cast_src=()):
    n, d = x.shape
    slabs = [_cast_slabs(w, layer, n // tm) for w in cast_src]
    d_ff = wg.shape[-1]
    d_in = w_in.shape[-1]
    weight = functools.partial(_layer_weight, layer=layer)
    row = lambda w: pl.BlockSpec((tm, w), lambda i: (i, 0))
    sds = lambda w, dt: jax.ShapeDtypeStruct((n, w), dt)
    layer_rows = pl.BlockSpec((None, tm, N_HEADS, HEAD_W), lambda i: (layer, i, 0, 0))
    in_hbm = pl.BlockSpec(memory_space=pl.ANY)
    return pl.pallas_call(
        functools.partial(_ffn_proj_kernel, alpha=alpha, d_attn=d_attn, d_rec=d_rec, n_cast=len(cast_src)),
        grid=(n // tm,),
        in_specs=[row(d), _resident((d, d_ff)), _resident((d, d_ff)), _resident((d_ff, d)),
                  _resident((1, d)), _resident((1, d)), weight((d, d_in)), in_hbm, in_hbm]
                 + [src for src, _ in slabs],
        out_specs=[row(d), row(d_attn), layer_rows, layer_rows, row(d_attn), row(d_attn), row(d_rec), row(d_rec)]
                  + [dst for _, dst in slabs],
        out_shape=[sds(d, F32), sds(d_attn, BF16), jax.ShapeDtypeStruct(k_all.shape, F32),
                   jax.ShapeDtypeStruct(v_all.shape, F32), sds(d_attn, BF16), sds(d_attn, BF16),
                   sds(d_rec, F32), sds(d_rec, F32)]
                  + [jax.ShapeDtypeStruct(w.shape[1:], BF16) for w in cast_src],
        input_output_aliases={7: 2, 8: 3},
        compiler_params=_compiler_params(("arbitrary",)),
        name="ffn_proj",
    )(x, wg, wu, wd, g, b, w_in, k_all, v_all, *cast_src)


def _mix_ffn_kernel(x_ref, attn_ref, rec_ref, w_out_ref, g2_ref, b2_ref, wg_ref, wu_ref, wd_ref, g3_ref, b3_ref,
                    *rest, alpha, n_cast):
    o_ref = rest[n_cast]
    _cast_weights(rest[:n_cast], rest[n_cast + 1:])
    groups = _row_groups(x_ref.shape[0])
    mixed = [jnp.concatenate([attn_ref[rows, :], rec_ref[rows, :]], axis=-1) for rows in groups]
    mixes = [_dot(m, w_out_ref[...]) for m in mixed]
    x2 = [_layer_norm(alpha * x_ref[rows, :] + mix, g2_ref[...], b2_ref[...]) for rows, mix in zip(groups, mixes)]
    for rows, out in zip(groups, _swiglu_ln(x2, wg_ref, wu_ref, wd_ref, g3_ref, b3_ref, alpha)):
        o_ref[rows, :] = out


def _mix_ffn(x, attn, rec, w_out, g2, b2, wg, wu, wd, g3, b3, *, layer, alpha, tm, cast_src=(), cast_layer=0):
    n, d = x.shape
    slabs = [_cast_slabs(w, cast_layer, n // tm) for w in cast_src]
    d_ff = wg.shape[-1]
    weight = functools.partial(_layer_weight, layer=layer)
    row = lambda w: pl.BlockSpec((tm, w), lambda i: (i, 0))
    vec_spec = _resident((1, d))
    return pl.pallas_call(
        functools.partial(_mix_ffn_kernel, alpha=alpha, n_cast=len(cast_src)),
        grid=(n // tm,),
        in_specs=[row(d), row(attn.shape[1]), row(rec.shape[1]), weight((d, d)), vec_spec, vec_spec,
                  _resident((d, d_ff)), _resident((d, d_ff)), _resident((d_ff, d)), vec_spec, vec_spec]
                 + [src for src, _ in slabs],
        out_specs=[row(d)] + [dst for _, dst in slabs],
        out_shape=[jax.ShapeDtypeStruct((n, d), F32)] + [jax.ShapeDtypeStruct(w.shape[1:], BF16) for w in cast_src],
        compiler_params=_compiler_params(("arbitrary",)),
        name="mix_ffn",
    )(x, attn, rec, w_out, g2, b2, wg, wu, wd, g3, b3, *cast_src)


def _lambda(lq1_ref, lk1_ref, lq2_ref, lk2_ref, lam_init):
    s1 = jnp.sum(lq1_ref[...] * lk1_ref[...], axis=-1, keepdims=True)
    s2 = jnp.sum(lq2_ref[...] * lk2_ref[...], axis=-1, keepdims=True)
    return jnp.exp(s1) - jnp.exp(s2) + lam_init


def _sub_ln(o, g, lam_init):
    ms = jnp.mean(o * o, axis=-1, keepdims=True)
    return o * lax.rsqrt(ms + LN_EPS) * g * (1.0 - lam_init)


def _stack_maps(q):
    lane = lax.broadcasted_iota(jnp.int32, q.shape, 1)
    zero = jnp.zeros_like(q)
    return jnp.concatenate([jnp.where(lane < HEAD_DIM, q, zero), jnp.where(lane >= HEAD_DIM, q, zero)], axis=0)


def _prompt_attn_kernel(lq1_ref, lk1_ref, lq2_ref, lk2_ref, g_ref, q_ref, k_ref, v_ref, o_ref, vx_ref,
                        *, lam_init, tq, nq, n_heads):
    qi = pl.program_id(2)

    @pl.when(qi == 0)
    def _():
        for h in range(n_heads):
            vx_ref[h, :, :HEAD_W] = v_ref[:, h * HEAD_W:(h + 1) * HEAD_W]
            vx_ref[h, :, HEAD_W:] = jnp.ones((vx_ref.shape[1], HEAD_W), BF16)

    lam = _lambda(lq1_ref, lk1_ref, lq2_ref, lk2_ref, lam_init)
    row = lax.broadcasted_iota(jnp.int32, (2 * tq, tq), 0)
    col = lax.broadcasted_iota(jnp.int32, (2 * tq, tq), 1)
    causal = col <= jnp.where(row >= tq, row - tq, row)

    def lane_chunks(s, n_keys):
        s_diag = jnp.where(causal, s[:, n_keys - tq:], NEG_INF)
        chunks = [s[:, c * LANES:(c + 1) * LANES] for c in range((n_keys - tq) // LANES)]
        return chunks + [s_diag[:, c * LANES:(c + 1) * LANES] for c in range(tq // LANES)]

    def row_max(chunks):
        m = jnp.max(functools.reduce(jnp.maximum, chunks), axis=-1, keepdims=True)
        return jnp.broadcast_to(m, (2 * tq, LANES))

    def finish(acc):
        o_all = acc[:, :HEAD_W] / acc[:, HEAD_W:]
        o = o_all[:tq] - lam * o_all[tq:]
        return _sub_ln(o, g_ref[...], lam_init).astype(o_ref.dtype)

    def attend(n_keys):
        lanes = [slice(h * HEAD_W, (h + 1) * HEAD_W) for h in range(n_heads)]
        qs = [_stack_maps(q_ref[:, ln]) for ln in lanes]
        scores = [lax.dot_general(q, k_ref[:n_keys, ln], (((1,), (1,)), ((), ())), preferred_element_type=F32)
                  for q, ln in zip(qs, lanes)]
        chunks = [lane_chunks(s, n_keys) for s in scores]
        maxes = [row_max(c) for c in chunks]
        ps = [jnp.concatenate([jnp.exp(c - m_b) for c in cs], axis=-1).astype(BF16) for cs, m_b in zip(chunks, maxes)]
        accs = [_dot(p, vx_ref[h, :n_keys, :]) for h, p in enumerate(ps)]
        o_ref[...] = jnp.concatenate([finish(acc) for acc in accs], axis=-1)

    for i in range(nq):
        pl.when(qi == i)(functools.partial(attend, (i + 1) * tq))


def _prompt_attn(q, kb, vb, lq1, lk1, lq2, lk2, g, *, lam_init, batch, seq, tq, heads_per_step):
    n, d_attn = q.shape
    nq = seq // tq
    width = heads_per_step * HEAD_W
    small = lambda a: _resident(a.shape)
    return pl.pallas_call(
        functools.partial(_prompt_attn_kernel, lam_init=lam_init, tq=tq, nq=nq, n_heads=heads_per_step),
        grid=(batch, N_HEADS // heads_per_step, nq),
        in_specs=[
            small(lq1), small(lk1), small(lq2), small(lk2), small(g),
            pl.BlockSpec((tq, width), lambda b, h, i: (b * nq + i, h)),
            pl.BlockSpec((seq, width), lambda b, h, i: (b, h)),
            pl.BlockSpec((seq, width), lambda b, h, i: (b, h)),
        ],
        out_specs=pl.BlockSpec((tq, width), lambda b, h, i: (b * nq + i, h)),
        out_shape=jax.ShapeDtypeStruct((n, d_attn), BF16),
        scratch_shapes=[pltpu.VMEM((heads_per_step, seq, 2 * HEAD_W), BF16)],
        compiler_params=_compiler_params(("parallel", "parallel", "arbitrary")),
        name="prompt_attn",
    )(lq1, lk1, lq2, lk2, g, q, kb, vb)


def _decode_attn_kernel(pt_ref, lq1_ref, lk1_ref, lq2_ref, lk2_ref, g_ref, q_ref, kn_ref, vn_ref, *rest,
                        lam_init, pages_per_step):
    del pt_ref
    k_refs = rest[:pages_per_step]
    v_refs = rest[pages_per_step:2 * pages_per_step]
    o_ref, m_ref, l_ref, acc_ref = rest[2 * pages_per_step:]
    step = pl.program_id(1)

    qm_f32 = _stack_maps(q_ref[...].astype(F32))
    qm = qm_f32.astype(BF16)

    @pl.when(step == 0)
    def _():
        m_ref[...] = jnp.full(m_ref.shape, NEG_INF, F32)
        l_ref[...] = jnp.zeros(l_ref.shape, F32)
        acc_ref[...] = jnp.zeros(acc_ref.shape, F32)

    def merge(m, l, acc):
        m_prev = m_ref[...]
        m_new = jnp.maximum(m_prev, m)
        corr = jnp.exp(m_prev - m_new)
        w = jnp.exp(m - m_new)
        l_ref[...] = corr * l_ref[...] + w * l
        acc_ref[...] = corr * acc_ref[...] + w * acc
        m_ref[...] = m_new

    k_cat = jnp.concatenate([r[...].astype(BF16) for r in k_refs], axis=0)
    v_cat = jnp.concatenate([r[...].astype(BF16) for r in v_refs], axis=0)
    s = lax.dot_general(qm, k_cat, (((1,), (1,)), ((), ())), preferred_element_type=F32)
    row = lax.broadcasted_iota(jnp.int32, s.shape, 0)
    col = lax.broadcasted_iota(jnp.int32, s.shape, 1)
    s = jnp.where(col % N_HEADS == row % N_HEADS, s, NEG_INF)
    m = jnp.max(s, axis=-1, keepdims=True)
    p = jnp.exp(s - m)
    merge(m, jnp.sum(p, axis=-1, keepdims=True), _dot(p.astype(BF16), v_cat))

    @pl.when(step == pl.num_programs(1) - 1)
    def _():
        kn = kn_ref[...].astype(BF16).astype(F32)
        vn = vn_ref[...].astype(BF16).astype(F32)
        s_self = jnp.sum(qm_f32 * jnp.concatenate([kn, kn], axis=0), axis=-1, keepdims=True)
        merge(s_self, jnp.ones_like(s_self), jnp.concatenate([vn, vn], axis=0))

        lam = _lambda(lq1_ref, lk1_ref, lq2_ref, lk2_ref, lam_init)
        o_all = acc_ref[...] / l_ref[...]
        o = o_all[:N_HEADS] - lam * o_all[N_HEADS:]
        o_ref[...] = _sub_ln(o, g_ref[...], lam_init).astype(o_ref.dtype)


def _decode_attn(q, k_new, v_new, cache_k, cache_v, page_table, lq1, lk1, lq2, lk2, g, *,
                 layer, lam_init, pages_per_step):
    n_seq, n_pages = page_table.shape
    page_rows = cache_k.shape[2]
    n_steps = n_pages // pages_per_step
    small = lambda a: pl.BlockSpec(a.shape, lambda s, p, pt: (0,) * a.ndim)
    per_seq = pl.BlockSpec((None, N_HEADS, HEAD_W), lambda s, p, pt: (s, 0, 0))

    def page_spec(j):
        return pl.BlockSpec((None, None, page_rows, HEAD_W),
                            lambda s, p, pt: (layer, pt[s, p * pages_per_step + j], 0, 0))

    pages = [page_spec(j) for j in range(pages_per_step)]
    return pl.pallas_call(
        functools.partial(_decode_attn_kernel, lam_init=lam_init, pages_per_step=pages_per_step),
        grid_spec=pltpu.PrefetchScalarGridSpec(
            num_scalar_prefetch=1,
            grid=(n_seq, n_steps),
            in_specs=[small(lq1), small(lk1), small(lq2), small(lk2), small(g), per_seq, per_seq, per_seq]
                     + pages + pages,
            out_specs=per_seq,
            scratch_shapes=[pltpu.VMEM((2 * N_HEADS, 1), F32), pltpu.VMEM((2 * N_HEADS, 1), F32),
                            pltpu.VMEM((2 * N_HEADS, HEAD_W), F32)],
        ),
        out_shape=jax.ShapeDtypeStruct((n_seq, N_HEADS, HEAD_W), BF16),
        compiler_params=_compiler_params(("parallel", "arbitrary")),
        name="decode_attn",
    )(page_table, lq1, lk1, lq2, lk2, g, q, k_new, v_new,
      *([cache_k] * pages_per_step), *([cache_v] * pages_per_step))


def _gelu_tanh(x):
    return 0.5 * x * (1.0 + jnp.tanh(math.sqrt(2.0 / math.pi) * (x + 0.044715 * (x * x * x))))


def _lru_coefficients(xc, wa_ref, ba_ref, wx_ref, bx_ref, lam_ref):
    xcb = xc.astype(BF16)
    r = jax.nn.sigmoid(_dot(xcb, wa_ref[...]) + ba_ref[...])
    i = jax.nn.sigmoid(_dot(xcb, wx_ref[...]) + bx_ref[...])
    log_a = -LRU_C * r * jax.nn.softplus(-lam_ref[...])
    a = jnp.exp(log_a)
    b = jnp.sqrt(-jnp.tanh(log_a) * (a * a + 1.0)) * i * xc
    return a, b


def _rglru_seq_kernel(xr_ref, xg_ref, conv0_ref, h0_ref, cw_ref, cb_ref, wa_ref, ba_ref, wx_ref, bx_ref, lam_ref,
                      y_ref, convn_ref, hn_ref, xs_ref, a_ref, b_ref, h_ref, *, tt, nb):
    t = pl.program_id(0)
    d_rec = xr_ref.shape[-1]
    hist = CONV_W - 1

    @pl.when(t == 0)
    def _():
        xs_ref[:hist] = conv0_ref[...]
        h_ref[...] = h0_ref[...]

    @pl.when(t > 0)
    def _():
        xs_ref[:hist] = xs_ref[tt:tt + hist]

    xs_ref[hist:] = jnp.swapaxes(xr_ref[...], 0, 1)
    xc = cb_ref[...][None]
    for j in range(CONV_W):
        xc = xc + xs_ref[j:j + tt] * cw_ref[j:j + 1, :][None]

    a, b = _lru_coefficients(xc.reshape(tt * nb, d_rec), wa_ref, ba_ref, wx_ref, bx_ref, lam_ref)
    a_ref[...] = a.reshape(tt, nb, d_rec)
    b_ref[...] = b.reshape(tt, nb, d_rec)

    def step(s, h):
        h = a_ref[s] * h + b_ref[s]
        b_ref[s] = h
        return h

    h = lax.fori_loop(0, tt, step, h_ref[...], unroll=8)
    h_ref[...] = h

    y = b_ref[...] * _gelu_tanh(jnp.swapaxes(xg_ref[...], 0, 1))
    y_ref[...] = jnp.swapaxes(y, 0, 1).astype(y_ref.dtype)
    convn_ref[...] = xs_ref[tt:tt + hist]
    hn_ref[...] = h


def _rglru_seq(xr, xg, conv0, h0, cw, cb, wa, ba, wx, bx, lam, *, tt):
    nb, seq, d_rec = xr.shape
    tile = pl.BlockSpec((nb, tt, d_rec), lambda t: (0, t, 0))
    whole = lambda a: pl.BlockSpec(a.shape, lambda t: (0,) * a.ndim)
    return pl.pallas_call(
        functools.partial(_rglru_seq_kernel, tt=tt, nb=nb),
        grid=(seq // tt,),
        in_specs=[tile, tile] + [whole(a) for a in (conv0, h0, cw, cb, wa, ba, wx, bx, lam)],
        out_specs=[tile, whole(conv0), whole(h0)],
        out_shape=[jax.ShapeDtypeStruct((nb, seq, d_rec), BF16),
                   jax.ShapeDtypeStruct(conv0.shape, F32), jax.ShapeDtypeStruct(h0.shape, F32)],
        scratch_shapes=[pltpu.VMEM((CONV_W - 1 + tt, nb, d_rec), F32), pltpu.VMEM((tt, nb, d_rec), F32),
                        pltpu.VMEM((tt, nb, d_rec), F32), pltpu.VMEM((nb, d_rec), F32)],
        compiler_params=_compiler_params(("arbitrary",)),
        name="rglru_seq",
    )(xr, xg, conv0, h0, cw, cb, wa, ba, wx, bx, lam)


def _rglru_step_kernel(xr_ref, xg_ref, conv_ref, h0_ref, cw_ref, cb_ref, wa_ref, ba_ref, wx_ref, bx_ref, lam_ref,
                       y_ref, convn_ref, hn_ref):
    xr = xr_ref[...]
    xc = cb_ref[...] + xr * cw_ref[CONV_W - 1:CONV_W, :]
    for j in range(CONV_W - 1):
        xc = xc + conv_ref[j] * cw_ref[j:j + 1, :]
    a, b = _lru_coefficients(xc, wa_ref, ba_ref, wx_ref, bx_ref, lam_ref)
    h = a * h0_ref[...] + b
    y_ref[...] = (h * _gelu_tanh(xg_ref[...])).astype(y_ref.dtype)
    for j in range(CONV_W - 2):
        convn_ref[j] = conv_ref[j + 1]
    convn_ref[CONV_W - 2] = xr
    hn_ref[...] = h


def _rglru_step(xr, xg, conv_t, h0, cw, cb, wa, ba, wx, bx, lam):
    args = (xr, xg, conv_t, h0, cw, cb, wa, ba, wx, bx, lam)
    whole = lambda a: pl.BlockSpec(a.shape, lambda: (0,) * a.ndim)
    return pl.pallas_call(
        _rglru_step_kernel,
        in_specs=[whole(a) for a in args],
        out_specs=[whole(xr), whole(conv_t), whole(h0)],
        out_shape=[jax.ShapeDtypeStruct(xr.shape, BF16), jax.ShapeDtypeStruct(conv_t.shape, F32),
                   jax.ShapeDtypeStruct(h0.shape, F32)],
        name="rglru_step",
    )(*args)


def _block_diag(w):
    n_blk, blk, _ = w.shape
    eye = jnp.eye(n_blk, dtype=w.dtype)
    return jnp.einsum("nde,nm->ndme", w, eye).reshape(n_blk * blk, n_blk * blk)


def _row_tile(n, preferred):
    return preferred if n % preferred == 0 else n


def kernel(x_prompt, x_sample, cache_k, cache_v, state_conv, state_h, page_table, ln1_g, ln1_b, ln2_g, ln2_b, ln3_g, ln3_b, ffn1_w_gate, ffn1_w_up, ffn1_w_down, ffn2_w_gate, ffn2_w_up, ffn2_w_down, w_in, w_out, attn_lam_q1, attn_lam_k1, attn_lam_q2, attn_lam_k2, attn_subln_g, conv_w, conv_b, gate_a_w, gate_a_b, gate_x_w, gate_x_b, lru_lambda):
    depth = w_in.shape[0]
    batch, seq, d_model = x_prompt.shape
    n_seq = x_sample.shape[0]
    assert x_sample.shape[1] == 1, "the sample group decodes one token per sequence"
    d_attn = N_HEADS * HEAD_W
    d_rec = d_model - d_attn
    alpha = (2 * depth) ** 0.25
    n_pool, page = cache_k.shape[1:3]
    n_tok = batch * seq

    cache_k2 = cache_k.reshape(depth, n_pool, page * N_HEADS, HEAD_W)
    cache_v2 = cache_v.reshape(depth, n_pool, page * N_HEADS, HEAD_W)
    bf = lambda w: w.astype(BF16)
    vec = lambda p, l: p[l][None, :]

    xp = x_prompt.reshape(n_tok, d_model)
    xs = x_sample.reshape(n_seq, d_model)
    conv_p = jnp.zeros((CONV_W - 1, batch, d_rec), F32)
    h_p = jnp.zeros((batch, d_rec), F32)
    kp_all = jnp.zeros((depth, n_tok, N_HEADS, HEAD_W), F32)
    vp_all = jnp.zeros((depth, n_tok, N_HEADS, HEAD_W), F32)
    ks_all = jnp.zeros((depth, n_seq, N_HEADS, HEAD_W), F32)
    vs_all = jnp.zeros((depth, n_seq, N_HEADS, HEAD_W), F32)

    tm = _row_tile(n_tok, 512)
    tq = _row_tile(seq, 256)
    tt = _row_tile(seq, 256)
    pages_per_step = next(p for p in (32, 16, 8, 4, 2, 1) if page_table.shape[1] % p == 0)

    ffn1_src = (ffn1_w_gate, ffn1_w_up, ffn1_w_down)
    ffn2_src = (ffn2_w_gate, ffn2_w_up, ffn2_w_down)
    ffn1_w = tuple(bf(w[0]) for w in ffn1_src)
    w_in_bf, w_out_bf = bf(w_in), bf(w_out)

    outs = {name: [] for name in ("cp", "hp", "cs", "hs")}
    for l in range(depth):
        lam_init = 0.8 - 0.6 * math.exp(-0.3 * l)
        ffn1 = (*ffn1_w, vec(ln1_g, l), vec(ln1_b, l))
        lam_vecs = (vec(attn_lam_q1, l), vec(attn_lam_k1, l), vec(attn_lam_q2, l), vec(attn_lam_k2, l))
        sub_g = vec(attn_subln_g, l)
        rec = (conv_w[l], vec(conv_b, l), bf(_block_diag(gate_a_w[l])), vec(gate_a_b, l),
               bf(_block_diag(gate_x_w[l])), vec(gate_x_b, l), vec(lru_lambda, l))

        xp, q, kp_all, vp_all, kb, vb, xr, xg, *ffn2_w = _ffn_proj(
            xp, *ffn1, w_in_bf, kp_all, vp_all, layer=l, alpha=alpha, d_attn=d_attn, d_rec=d_rec, tm=tm,
            cast_src=ffn2_src)
        ffn2 = (*ffn2_w, vec(ln3_g, l), vec(ln3_b, l))
        attn = _prompt_attn(q, kb, vb, *lam_vecs, sub_g, lam_init=lam_init, batch=batch, seq=seq, tq=tq,
                            heads_per_step=N_HEADS)
        y, conv_n, h_n = _rglru_seq(xr.reshape(batch, seq, d_rec), xg.reshape(batch, seq, d_rec),
                                    conv_p, h_p, *rec, tt=tt)
        xp, *ffn1_next = _mix_ffn(xp, attn, y.reshape(n_tok, d_rec), w_out_bf, vec(ln2_g, l), vec(ln2_b, l), *ffn2,
                                  layer=l, alpha=alpha, tm=tm,
                                  cast_src=ffn1_src if l + 1 < depth else (), cast_layer=l + 1)
        outs["cp"].append(jnp.swapaxes(conv_n, 0, 1))
        outs["hp"].append(h_n)

        xs, q, ks_all, vs_all, _, _, xr, xg = _ffn_proj(xs, *ffn1, w_in_bf, ks_all, vs_all, layer=l, alpha=alpha,
                                                        d_attn=d_attn, d_rec=d_rec, tm=n_seq)
        attn = _decode_attn(q.reshape(n_seq, N_HEADS, HEAD_W), ks_all[l], vs_all[l], cache_k2, cache_v2, page_table,
                            *lam_vecs, sub_g, layer=l, lam_init=lam_init, pages_per_step=pages_per_step)
        y, conv_n, h_n = _rglru_step(xr, xg, jnp.swapaxes(state_conv[l], 0, 1), state_h[l], *rec)
        (xs,) = _mix_ffn(xs, attn.reshape(n_seq, d_attn), y, w_out_bf, vec(ln2_g, l), vec(ln2_b, l), *ffn2,
                         layer=l, alpha=alpha, tm=n_seq)
        outs["cs"].append(jnp.swapaxes(conv_n, 0, 1))
        outs["hs"].append(h_n)
        ffn1_w = tuple(ffn1_next) or ffn1_w

    stack = lambda name: jnp.stack(outs[name])
    return (xp.reshape(batch, seq, d_model), xs.reshape(n_seq, 1, d_model),
            kp_all.reshape(depth, batch, seq, N_HEADS, HEAD_W), vp_all.reshape(depth, batch, seq, N_HEADS, HEAD_W),
            stack("cp"), stack("hp"),
            ks_all.reshape(depth, n_seq, 1, N_HEADS, HEAD_W), vs_all.reshape(depth, n_seq, 1, N_HEADS, HEAD_W),
            stack("cs"), stack("hs"))
```
